```python
import math
import jax
import jax.numpy as jnp
from jax import lax
import numpy as np

D_MODEL = 2048
BATCH = 1
SEQ = 16384
DEPTH = 4
DEC_BATCH = 32
DEC_SEQ = 32
PAST_LEN = 4096

CHUNK = 64
EPS = 1e-6
CONV_W = 3
W_CONV = 1024
W_SSM = 1024
SSM_GROUP = 16
SSM_GROUPS = W_SSM // SSM_GROUP
SSM_STATE = 64
SSM_SCAN_BLOCK = 1024
W_SGU = 1024
SGU_CHUNK = 128
SGU_HEADS = 8
SGU_HEAD_DIM = W_SGU // SGU_HEADS
N_BRANCH = 3
D_FF = -(-8 * D_MODEL // (3 * 256)) * 256
SPLITS = (W_CONV, 2 * W_CONV, 3 * W_CONV, 3 * W_CONV + W_SSM, 3 * W_CONV + W_SSM + W_SGU)
D_IN = 3 * W_CONV + W_SSM + 2 * W_SGU

kernel_name = 'hybrid_streaming_encoder_step'


def rms_norm(x, g):
    x32 = x.astype(jnp.float32)
    y = x32 * lax.rsqrt(jnp.mean(x32 * x32, axis=-1, keepdims=True) + EPS)
    return (y * g.astype(jnp.float32)).astype(x.dtype)


def layer_norm(x, g):
    x32 = x.astype(jnp.float32)
    xc = x32 - jnp.mean(x32, axis=-1, keepdims=True)
    y = xc * lax.rsqrt(jnp.mean(xc * xc, axis=-1, keepdims=True) + EPS)
    return (y * g.astype(jnp.float32)).astype(x.dtype)


def _cplx(re, im):
    return lax.complex(re.astype(jnp.float32), im.astype(jnp.float32))


def short_conv(z, buf, w):
    L = z.shape[1]
    zp = jnp.concatenate([buf.astype(z.dtype), z], axis=1)
    y = sum(w[k] * zp[:, k:k + L] for k in range(CONV_W))
    return y, zp[:, L:]


def s5_discretize(lam_re, lam_im, log_dt, b_re, b_im):
    lam = _cplx(lam_re, lam_im)
    dt = jnp.exp(log_dt.astype(jnp.float32))[:, None]
    lam_bar = jnp.exp(lam * dt)
    b_bar = ((lam_bar - 1.0) / lam)[:, :, None] * _cplx(b_re, b_im)
    return lam_bar, b_bar


def _affine_combine(left, right):
    a_l, b_l = left
    a_r, b_r = right
    return a_r * a_l, a_r * b_l + b_r


def s5_scan(u, s0, lam_bar, b_bar, c, d):
    nb, L, _ = u.shape
    blk = math.gcd(L, SSM_SCAN_BLOCK)
    ub = u.reshape(nb, L // blk, blk, SSM_GROUPS, SSM_GROUP).transpose(1, 0, 2, 3, 4)

    def block_step(s, u_blk):
        bu = jnp.einsum('blgi,gpi->blgp', u_blk.astype(jnp.complex64), b_bar)
        a = jnp.broadcast_to(lam_bar, bu.shape)
        a_cum, h = lax.associative_scan(_affine_combine, (a, bu), axis=1)
        h = h + a_cum * s[:, None]
        y = jnp.einsum('blgp,gip->blgi', h, c).real
        return h[:, -1], y

    s_last, y = lax.scan(block_step, s0, ub)
    y = y.transpose(1, 0, 2, 3, 4).reshape(nb, L, W_SSM)
    return y + d.astype(jnp.float32) * u, s_last


def spatial_gate(u, v, w_s, b_s):
    nb, L, _ = v.shape
    n = -(-L // SGU_CHUNK)
    vp = jnp.pad(v, ((0, 0), (0, n * SGU_CHUNK - L), (0, 0)))
    vp = vp.reshape(nb, n, SGU_CHUNK, SGU_HEADS, SGU_HEAD_DIM)
    mask = jnp.tril(jnp.ones((SGU_CHUNK, SGU_CHUNK), dtype=bool))
    w = jnp.where(mask, w_s, 0)
    mix = jnp.einsum('hts,bnshd->bnthd', w, vp) + b_s.T[:, :, None]
    mix = mix.reshape(nb, n * SGU_CHUNK, W_SGU)[:, :L]
    return u * mix


def mixer(xn, conv_buf, s0, w_in, conv_w, w_conv_out, lam_bar, b_bar, c_ssm, d_ssm,
          w_glu, b_glu, w_ssm_out, ln_v_g, w_sgu_s, b_sgu_s, w_sgu_out, w_gate, b_gate, w_o):
    h = xn @ w_in
    b_g, c_g, h_c, u_s, u_g, v_g = jnp.split(h, SPLITS, axis=-1)
    z_conv, conv_new = short_conv(c_g * h_c, conv_buf, conv_w)
    y_a = (b_g * z_conv) @ w_conv_out
    y_s, s_new = s5_scan(u_s.astype(jnp.float32), s0, lam_bar, b_bar, c_ssm, d_ssm)
    y_s = jax.nn.gelu(y_s).astype(xn.dtype)
    y_s = y_s * jax.nn.sigmoid(y_s @ w_glu + b_glu)
    y_b = y_s @ w_ssm_out
    v_n = layer_norm(jax.nn.gelu(v_g), ln_v_g)
    y_c = spatial_gate(jax.nn.gelu(u_g), v_n, w_sgu_s, b_sgu_s) @ w_sgu_out
    gates = jax.nn.sigmoid(jnp.einsum('bld,dke->blke', xn, w_gate) + b_gate)
    merged = gates[:, :, 0] * y_a + gates[:, :, 1] * y_b + gates[:, :, 2] * y_c
    return merged @ w_o, conv_new, s_new, v_n


def swiglu(xn, w_ffn_in, w_ffn_out):
    g, u = jnp.split(xn @ w_ffn_in, 2, axis=-1)
    return (jax.nn.silu(g) * u) @ w_ffn_out


def setup_inputs(seed: int = 0) -> dict:
    key = jax.random.key(seed)
    ks = iter(jax.random.split(key, 40))
    f32 = jnp.float32

    def nrm(shape, scale):
        return jax.random.normal(next(ks), shape, f32) * scale

    n_idx = jnp.arange(SSM_STATE, dtype=f32)
    return {
        'x_prompt': nrm((BATCH, SEQ, D_MODEL), 1.0),
        'x_sample': nrm((DEC_BATCH, DEC_SEQ, D_MODEL), 1.0),
        'cache_conv': nrm((DEPTH, DEC_BATCH, CONV_W - 1, W_CONV), 1.0),
        'state_ssm_re': nrm((DEPTH, DEC_BATCH, SSM_GROUPS, SSM_STATE), 0.5),
        'state_ssm_im': nrm((DEPTH, DEC_BATCH, SSM_GROUPS, SSM_STATE), 0.5),
        'norm_mix_g': 1.0 + nrm((DEPTH, D_MODEL), 0.01),
        'w_in': nrm((DEPTH, D_MODEL, D_IN), D_MODEL ** -0.5),
        'conv_w': nrm((DEPTH, CONV_W, W_CONV), 0.5),
        'w_conv_out': nrm((DEPTH, W_CONV, D_MODEL), W_CONV ** -0.5),
        'ssm_lam_re': -0.5 + nrm((DEPTH, SSM_GROUPS, SSM_STATE), 0.01),
        'ssm_lam_im': jnp.pi * n_idx + nrm((DEPTH, SSM_GROUPS, SSM_STATE), 0.01),
        'ssm_log_dt': jax.random.uniform(next(ks), (DEPTH, SSM_GROUPS), f32, math.log(1e-3), math.log(1e-1)),
        'ssm_b_re': nrm((DEPTH, SSM_GROUPS, SSM_STATE, SSM_GROUP), (2 * SSM_GROUP) ** -0.5),
        'ssm_b_im': nrm((DEPTH, SSM_GROUPS, SSM_STATE, SSM_GROUP), (2 * SSM_GROUP) ** -0.5),
        'ssm_c_re': nrm((DEPTH, SSM_GROUPS, SSM_GROUP, SSM_STATE), (2 * SSM_STATE) ** -0.5),
        'ssm_c_im': nrm((DEPTH, SSM_GROUPS, SSM_GROUP, SSM_STATE), (2 * SSM_STATE) ** -0.5),
        'ssm_d': nrm((DEPTH, W_SSM), 1.0),
        'w_glu': nrm((DEPTH, W_SSM, W_SSM), W_SSM ** -0.5),
        'b_glu': nrm((DEPTH, W_SSM), 0.01),
        'w_ssm_out': nrm((DEPTH, W_SSM, D_MODEL), W_SSM ** -0.5),
        'ln_v_g': 1.0 + nrm((DEPTH, W_SGU), 0.01),
        'w_sgu_s': nrm((DEPTH, SGU_HEADS, SGU_CHUNK, SGU_CHUNK), 0.5 * SGU_CHUNK ** -0.5),
        'b_sgu_s': 1.0 + nrm((DEPTH, SGU_HEADS, SGU_CHUNK), 0.01),
        'w_sgu_out': nrm((DEPTH, W_SGU, D_MODEL), W_SGU ** -0.5),
        'w_gate': nrm((DEPTH, D_MODEL, N_BRANCH, D_MODEL), D_MODEL ** -0.5),
        'b_gate': nrm((DEPTH, N_BRANCH, D_MODEL), 0.01),
        'w_o': nrm((DEPTH, D_MODEL, D_MODEL), D_MODEL ** -0.5),
        'norm_ffn_g': 1.0 + nrm((DEPTH, D_MODEL), 0.01),
        'w_ffn_in': nrm((DEPTH, D_MODEL, 2 * D_FF), D_MODEL ** -0.5),
        'w_ffn_out': nrm((DEPTH, D_FF, D_MODEL), D_FF ** -0.5),
        'norm_final_g': 1.0 + nrm((D_MODEL,), 0.01),
    }


def reference(x_prompt, x_sample, cache_conv, state_ssm_re, state_ssm_im, norm_mix_g, w_in, conv_w,
              w_conv_out, ssm_lam_re, ssm_lam_im, ssm_log_dt, ssm_b_re, ssm_b_im, ssm_c_re, ssm_c_im,
              ssm_d, w_glu, b_glu, w_ssm_out, ln_v_g, w_sgu_s, b_sgu_s, w_sgu_out, w_gate, b_gate, w_o,
              norm_ffn_g, w_ffn_in, w_ffn_out, norm_final_g):
    xp, xs = x_prompt, x_sample
    nbp = x_prompt.shape[0]
    conv_p, re_p, im_p, conv_s, re_s, im_s, v_s = [], [], [], [], [], [], []
    for l in range(DEPTH):
        lam_bar, b_bar = s5_discretize(ssm_lam_re[l], ssm_lam_im[l], ssm_log_dt[l], ssm_b_re[l], ssm_b_im[l])
        c_ssm = _cplx(ssm_c_re[l], ssm_c_im[l])

        def layer(x, conv_buf, s0):
            out, conv_new, s_new, v_n = mixer(
                rms_norm(x, norm_mix_g[l]), conv_buf, s0, w_in[l], conv_w[l], w_conv_out[l],
                lam_bar, b_bar, c_ssm, ssm_d[l], w_glu[l], b_glu[l], w_ssm_out[l], ln_v_g[l],
                w_sgu_s[l], b_sgu_s[l], w_sgu_out[l], w_gate[l], b_gate[l], w_o[l])
            x = x + out
            x = x + swiglu(rms_norm(x, norm_ffn_g[l]), w_ffn_in[l], w_ffn_out[l])
            return x, conv_new, s_new, v_n

        xp, cp, sp, _ = layer(xp, jnp.zeros((nbp, CONV_W - 1, W_CONV), xp.dtype),
                              jnp.zeros((nbp, SSM_GROUPS, SSM_STATE), jnp.complex64))
        xs, cs, ss, vs = layer(xs, cache_conv[l], _cplx(state_ssm_re[l], state_ssm_im[l]))
        conv_p.append(cp)
        re_p.append(sp.real)
        im_p.append(sp.imag)
        conv_s.append(cs)
        re_s.append(ss.real)
        im_s.append(ss.imag)
        v_s.append(vs)

    y_prompt = rms_norm(xp, norm_final_g)
    y_sample = rms_norm(xs, norm_final_g)
    sdt = state_ssm_re.dtype
    new_conv_prompt = jnp.stack(conv_p)
    new_ssm_re_prompt = jnp.stack(re_p).astype(sdt)
    new_ssm_im_prompt = jnp.stack(im_p).astype(sdt)
    new_conv_sample = jnp.stack(conv_s)
    new_ssm_re_sample = jnp.stack(re_s).astype(sdt)
    new_ssm_im_sample = jnp.stack(im_s).astype(sdt)
    new_sgu_v_sample = jnp.stack(v_s)
    return (y_prompt, y_sample, new_conv_prompt, new_ssm_re_prompt, new_ssm_im_prompt,
            new_conv_sample, new_ssm_re_sample, new_ssm_im_sample, new_sgu_v_sample)
```

```python
import functools

import jax
import jax.numpy as jnp
from jax import lax
from jax.experimental import pallas as pl
from jax.experimental.pallas import tpu as pltpu

F32 = jnp.float32
BF16 = jnp.bfloat16

D_MODEL = 2048
DEPTH = 4
W_BR = 1024
N_Q = 8
Q_IN = 128
Q_ST = 512
SGU_HEADS = 8
HEAD_DIM = 128
D_FF = 5632
EPS = 1e-6
BLK = 1024
SUBLANES = 8
VMEM_LIMIT_BYTES = 56 * 1024 * 1024


def _params(*sem):
    return pltpu.CompilerParams(dimension_semantics=sem, vmem_limit_bytes=VMEM_LIMIT_BYTES)


def _gelu(x):
    return jax.nn.gelu(x, approximate=True)


def _norm_mm_kernel(x_ref, g_ref, w_ref, b_ref, o_ref, xn_ref, *, sigmoid):
    @pl.when(pl.program_id(1) == 0)
    def _():
        x = x_ref[...]
        xn = x * lax.rsqrt(jnp.mean(x * x, axis=-1, keepdims=True) + EPS) * g_ref[...]
        xn_ref[...] = xn.astype(BF16)

    acc = jnp.dot(xn_ref[...], w_ref[...], preferred_element_type=F32) + b_ref[...]
    o_ref[...] = jax.nn.sigmoid(acc) if sigmoid else acc


def _norm_matmul_planes(x, g, w, b, plane_w, *, sigmoid, name):
    m = x.shape[0]
    n = w.shape[1]
    tm, tn = BLK, 1024
    per = plane_w // tn
    return pl.pallas_call(
        functools.partial(_norm_mm_kernel, sigmoid=sigmoid),
        grid=(m // tm, n // tn),
        in_specs=[
            pl.BlockSpec((tm, D_MODEL), lambda i, j: (i, 0)),
            pl.BlockSpec((1, D_MODEL), lambda i, j: (0, 0)),
            pl.BlockSpec((D_MODEL, tn), lambda i, j: (0, j)),
            pl.BlockSpec((1, tn), lambda i, j: (0, j)),
        ],
        out_specs=pl.BlockSpec((None, tm, tn), lambda i, j: (j // per, i, j % per)),
        out_shape=jax.ShapeDtypeStruct((n // plane_w, m, plane_w), F32),
        scratch_shapes=[pltpu.VMEM((tm, D_MODEL), BF16)],
        compiler_params=_params("parallel", "arbitrary"),
        name=name,
    )(x, g, w, b)


def _conv_taps(z, b1, b2, w, r):
    prev1 = jnp.concatenate([b1, z[:BLK - r]], axis=0)
    prev2 = jnp.concatenate([b2, b1, z[:BLK - 2 * r]], axis=0)
    return w[0:1] * prev2 + w[1:2] * prev1 + w[2:3] * z


def _conv_prompt_kernel(b_ref, c_ref, hc_ref, w_ref, ya_ref, cn_ref, carry_ref):
    r = SUBLANES

    @pl.when(pl.program_id(1) == 0)
    def _():
        carry_ref[...] = jnp.zeros_like(carry_ref)

    z = c_ref[...] * hc_ref[...]
    last1 = z[BLK - r:]
    last2 = z[BLK - 2 * r:BLK - r]
    row = lax.broadcasted_iota(jnp.int32, last1.shape, 0)
    b1 = jnp.where(row == 0, carry_ref[1:2, :], pltpu.roll(last1, 1, 0))
    b2 = jnp.where(row == 0, carry_ref[0:1, :], pltpu.roll(last2, 1, 0))
    y = _conv_taps(z, b1, b2, w_ref[...], r)
    ya_ref[...] = (b_ref[...] * y).astype(BF16)
    for k, last in enumerate((last2, last1)):
        carry_ref[k:k + 1, :] = last[r - 1:r]
        cn_ref[k:k + 1, :] = last[r - 1:r]


def _conv_sample_kernel(b_ref, c_ref, hc_ref, w_ref, buf0_ref, buf1_ref, ya_ref, cn_ref):
    r = 32
    z = c_ref[...] * hc_ref[...]
    y = _conv_taps(z, buf1_ref[...], buf0_ref[...], w_ref[...], r)
    ya_ref[...] = (b_ref[...] * y).astype(BF16)
    cn_ref[...] = z[BLK - 2 * r:]


def _conv_branch(h3, conv_w, bufs, *, prompt):
    m = h3.shape[1]
    tc = 256
    plane = lambda k: pl.BlockSpec((None, BLK, tc), lambda c, i, k=k: (k, i, c))
    w_spec = pl.BlockSpec((3, tc), lambda c, i: (0, c))
    ya_spec = pl.BlockSpec((BLK, tc), lambda c, i: (i, c))
    if prompt:
        return pl.pallas_call(
            _conv_prompt_kernel,
            grid=(W_BR // tc, m // BLK),
            in_specs=[plane(0), plane(1), plane(2), w_spec],
            out_specs=[ya_spec, pl.BlockSpec((2, tc), lambda c, i: (0, c))],
            out_shape=[jax.ShapeDtypeStruct((m, W_BR), BF16), jax.ShapeDtypeStruct((2, W_BR), F32)],
            scratch_shapes=[pltpu.VMEM((SUBLANES, tc), F32)],
            compiler_params=_params("parallel", "arbitrary"),
            name="conv_prompt",
        )(h3, h3, h3, conv_w)
    buf_spec = pl.BlockSpec((32, tc), lambda c, i: (0, c))
    return pl.pallas_call(
        _conv_sample_kernel,
        grid=(W_BR // tc, 1),
        in_specs=[plane(0), plane(1), plane(2), w_spec, buf_spec, buf_spec],
        out_specs=[ya_spec, pl.BlockSpec((64, tc), lambda c, i: (0, c))],
        out_shape=[jax.ShapeDtypeStruct((m, W_BR), BF16), jax.ShapeDtypeStruct((64, W_BR), F32)],
        compiler_params=_params("parallel", "arbitrary"),
        name="conv_sample",
    )(h3, h3, h3, conv_w, bufs[0], bufs[1])


def _scan_rows(hre, him, lr, li, init, base, stride, steps):
    def step(j, st):
        sr, si = st
        r0 = pl.multiple_of(base + j * stride, SUBLANES)
        nr = lr * sr - li * si + hre[pl.ds(r0, SUBLANES), :]
        ni = lr * si + li * sr + him[pl.ds(r0, SUBLANES), :]
        hre[pl.ds(r0, SUBLANES), :] = nr
        him[pl.ds(r0, SUBLANES), :] = ni
        return nr, ni

    return lax.fori_loop(0, steps, step, init, unroll=4)


def _s5_project_in(u_ref, wb_ref, hre, him):
    bu = jnp.dot(u_ref[...].astype(BF16), wb_ref[...], preferred_element_type=F32)
    hre[...] = bu[:, :Q_ST]
    him[...] = bu[:, Q_ST:]


def _s5_project_out(u_ref, wc_ref, d_ref, y_ref, hre, him):
    h = jnp.concatenate([hre[...], him[...]], axis=1).astype(BF16)
    y_ref[...] = jnp.dot(h, wc_ref[...], preferred_element_type=F32) + d_ref[...] * u_ref[...]


def _s5_prompt_kernel(u_ref, wb_ref, lam_ref, lams_ref, wc_ref, d_ref, y_ref, sfin_ref, hre, him, carry):
    @pl.when(pl.program_id(1) == 0)
    def _():
        carry[...] = jnp.zeros_like(carry)

    _s5_project_in(u_ref, wb_ref, hre, him)
    tile = (SUBLANES, Q_ST)
    lr = jnp.broadcast_to(lam_ref[0:1, :], tile)
    li = jnp.broadcast_to(lam_ref[1:2, :], tile)
    zero = jnp.zeros(tile, F32)
    steps = BLK // SUBLANES
    er, ei = _scan_rows(hre, him, lr, li, (zero, zero), 0, SUBLANES, steps)
    pr, pi = lams_ref[0:1, :], lams_ref[1:2, :]
    cr, ci = carry[0:1, :], carry[1:2, :]
    row = lax.broadcasted_iota(jnp.int32, tile, 0)
    sin_r, sin_i = zero, zero
    for s in range(SUBLANES):
        sin_r = jnp.where(row == s, cr, sin_r)
        sin_i = jnp.where(row == s, ci, sin_i)
        cr, ci = er[s:s + 1] + pr * cr - pi * ci, ei[s:s + 1] + pr * ci + pi * cr
    carry[0:1, :] = cr
    carry[1:2, :] = ci
    sfin_ref[0:1, :] = cr
    sfin_ref[1:2, :] = ci

    def fix(j, e):
        fr, fi = e
        nr = lr * fr - li * fi
        ni = lr * fi + li * fr
        r0 = pl.multiple_of(j * SUBLANES, SUBLANES)
        hre[pl.ds(r0, SUBLANES), :] += nr
        him[pl.ds(r0, SUBLANES), :] += ni
        return nr, ni

    lax.fori_loop(0, steps, fix, (sin_r, sin_i), unroll=4)
    _s5_project_out(u_ref, wc_ref, d_ref, y_ref, hre, him)


def _s5_sample_kernel(u_ref, wb_ref, lam_ref, wc_ref, d_ref, s0r_ref, s0i_ref, y_ref, sr_ref, si_ref, hre, him):
    _s5_project_in(u_ref, wb_ref, hre, him)
    tile = (SUBLANES, Q_ST)
    lr = jnp.broadcast_to(lam_ref[0:1, :], tile)
    li = jnp.broadcast_to(lam_ref[1:2, :], tile)
    for r in range(32 // SUBLANES):
        rows = slice(r * SUBLANES, (r + 1) * SUBLANES)
        er, ei = _scan_rows(hre, him, lr, li, (s0r_ref[rows, :], s0i_ref[rows, :]), r * SUBLANES, 32, 32)
        sr_ref[rows, :] = er
        si_ref[rows, :] = ei
    _s5_project_out(u_ref, wc_ref, d_ref, y_ref, hre, him)


def _s5_branch(h3, tabs, s0, *, prompt):
    m = h3.shape[1]
    u_spec = pl.BlockSpec((None, BLK, Q_IN), lambda q, i: (3, i, q))
    wb_spec = pl.BlockSpec((None, Q_IN, 2 * Q_ST), lambda q, i: (q, 0, 0))
    lam_spec = pl.BlockSpec((None, 2, Q_ST), lambda q, i: (q, 0, 0))
    wc_spec = pl.BlockSpec((None, 2 * Q_ST, Q_IN), lambda q, i: (q, 0, 0))
    d_spec = pl.BlockSpec((1, Q_IN), lambda q, i: (0, q))
    y_spec = pl.BlockSpec((BLK, Q_IN), lambda q, i: (i, q))
    scratch = [pltpu.VMEM((BLK, Q_ST), F32), pltpu.VMEM((BLK, Q_ST), F32)]
    if prompt:
        return pl.pallas_call(
            _s5_prompt_kernel,
            grid=(N_Q, m // BLK),
            in_specs=[u_spec, wb_spec, lam_spec, lam_spec, wc_spec, d_spec],
            out_specs=[y_spec, pl.BlockSpec((None, 2, Q_ST), lambda q, i: (q, 0, 0))],
            out_shape=[jax.ShapeDtypeStruct((m, W_BR), F32), jax.ShapeDtypeStruct((N_Q, 2, Q_ST), F32)],
            scratch_shapes=scratch + [pltpu.VMEM((SUBLANES, Q_ST), F32)],
            compiler_params=_params("parallel", "arbitrary"),
            name="s5_prompt",
        )(h3, tabs["wb"], tabs["lam"], tabs["lam_seg"], tabs["wc"], tabs["d"])
    st_spec = pl.BlockSpec((32, Q_ST), lambda q, i: (0, q))
    return pl.pallas_call(
        _s5_sample_kernel,
        grid=(N_Q, 1),
        in_specs=[u_spec, wb_spec, lam_spec, wc_spec, d_spec, st_spec, st_spec],
        out_specs=[y_spec, st_spec, st_spec],
        out_shape=[jax.ShapeDtypeStruct((m, W_BR), F32)] + [jax.ShapeDtypeStruct((32, N_Q * Q_ST), F32)] * 2,
        scratch_shapes=scratch,
        compiler_params=_params("parallel", "arbitrary"),
        name="s5_sample",
    )(h3, tabs["wb"], tabs["lam"], tabs["wc"], tabs["d"], s0[0], s0[1])


def _s5_tables(lam_re, lam_im, log_dt, b_re, b_im, c_re, c_im, d):
    dt = jnp.exp(log_dt)[:, None]
    mag = jnp.exp(lam_re * dt)
    lbr, lbi = mag * jnp.cos(lam_im * dt), mag * jnp.sin(lam_im * dt)
    den = lam_re * lam_re + lam_im * lam_im
    qr = ((lbr - 1.0) * lam_re + lbi * lam_im) / den
    qi = (lbi * lam_re - (lbr - 1.0) * lam_im) / den
    bbr = qr[:, :, None] * b_re - qi[:, :, None] * b_im
    bbi = qr[:, :, None] * b_im + qi[:, :, None] * b_re
    eye = jnp.eye(N_Q, dtype=F32)

    def block_in(b):
        return jnp.einsum("ab,qapi->qaibp", eye, b.reshape(N_Q, N_Q, 64, 16)).reshape(N_Q, Q_IN, Q_ST)

    def block_out(c):
        return jnp.einsum("ab,qaip->qapbi", eye, c.reshape(N_Q, N_Q, 16, 64)).reshape(N_Q, Q_ST, Q_IN)

    wb = jnp.concatenate([block_in(bbr), block_in(bbi)], axis=2).astype(BF16)
    wc = jnp.concatenate([block_out(c_re), -block_out(c_im)], axis=1).astype(BF16)
    sr, si = lbr, lbi
    for _ in range(7):
        sr, si = sr * sr - si * si, 2.0 * sr * si
    pack = lambda a, b: jnp.stack([a.reshape(N_Q, Q_ST), b.reshape(N_Q, Q_ST)], axis=1)
    return {"wb": wb, "wc": wc, "lam": pack(lbr, lbi), "lam_seg": pack(sr, si), "d": d.reshape(1, W_BR)}


def _sgu_kernel(u_ref, v_ref, lng_ref, w_ref, bs_ref, yc_ref, *rest, r):
    vn_ref = rest[0] if len(rest) == 2 else None
    vnb_ref = rest[-1]
    lng = lng_ref[...]
    for s in range(r):
        cols = slice(s * W_BR, (s + 1) * W_BR)
        gv = _gelu(v_ref[:, cols])
        xc = gv - jnp.mean(gv, axis=-1, keepdims=True)
        vn = xc * lax.rsqrt(jnp.mean(xc * xc, axis=-1, keepdims=True) + EPS) * lng
        if vn_ref is not None:
            vn_ref[:, cols] = vn
        vnb_ref[:, cols] = vn.astype(BF16)
    for hd in range(SGU_HEADS):
        strips = [slice(s * W_BR + hd * HEAD_DIM, s * W_BR + (hd + 1) * HEAD_DIM) for s in range(r)]
        vh = jnp.concatenate([vnb_ref[:, c] for c in strips], axis=1)
        mix = jnp.dot(w_ref[hd], vh, preferred_element_type=F32) + bs_ref[hd]
        for s, c in enumerate(strips):
            yc_ref[:, c] = (_gelu(u_ref[:, c]) * mix[:, s * HEAD_DIM:(s + 1) * HEAD_DIM]).astype(BF16)


def _sgu_branch(h3, ln_g, w_tril, b_s, *, prompt):
    m = h3.shape[1]
    r = SUBLANES if prompt else 32
    j = BLK // r
    hv = h3.reshape(6, m // r, r * W_BR)
    out_shape = [jax.ShapeDtypeStruct((m // r, r * W_BR), BF16)]
    row_spec = lambda: pl.BlockSpec((j, r * W_BR), lambda i: (i, 0))
    out_specs = [row_spec()]
    if not prompt:
        out_shape.append(jax.ShapeDtypeStruct((m // r, r * W_BR), F32))
        out_specs.append(row_spec())
    res = pl.pallas_call(
        functools.partial(_sgu_kernel, r=r),
        grid=(m // BLK,),
        in_specs=[
            pl.BlockSpec((None, j, r * W_BR), lambda i: (4, i, 0)),
            pl.BlockSpec((None, j, r * W_BR), lambda i: (5, i, 0)),
            pl.BlockSpec((1, W_BR), lambda i: (0, 0)),
            pl.BlockSpec((SGU_HEADS, j, j), lambda i: (0, 0, 0)),
            pl.BlockSpec((SGU_HEADS, j, 1), lambda i: (0, 0, 0)),
        ],
        out_specs=out_specs,
        out_shape=out_shape,
        scratch_shapes=[pltpu.VMEM((j, r * W_BR), BF16)],
        compiler_params=_params("parallel"),
        name="sgu_prompt" if prompt else "sgu_sample",
    )(hv, hv, ln_g, w_tril[:, :j, :j], b_s[:, :j, None])
    yc = res[0].reshape(m, W_BR)
    return (yc, None) if prompt else (yc, res[1])


def _glu_kernel(ys_ref, w_ref, b_ref, o_ref):
    g = _gelu(ys_ref[...])
    gate = jax.nn.sigmoid(jnp.dot(g.astype(BF16), w_ref[...], preferred_element_type=F32) + b_ref[...])
    o_ref[...] = (g * gate).astype(BF16)


def _glu(ys, w, b):
    m = ys.shape[0]
    return pl.pallas_call(
        _glu_kernel,
        grid=(m // BLK,),
        in_specs=[
            pl.BlockSpec((BLK, W_BR), lambda i: (i, 0)),
            pl.BlockSpec((W_BR, W_BR), lambda i: (0, 0)),
            pl.BlockSpec((1, W_BR), lambda i: (0, 0)),
        ],
        out_specs=pl.BlockSpec((BLK, W_BR), lambda i: (i, 0)),
        out_shape=jax.ShapeDtypeStruct((m, W_BR), BF16),
        compiler_params=_params("parallel"),
        name="glu",
    )(ys, w, b)


def _merge_kernel(ya_ref, yb_ref, yc_ref, wa_ref, wb_ref, wc_ref, g0_ref, g1_ref, g2_ref, o_ref):
    dot = lambda a, w: jnp.dot(a[...], w[...], preferred_element_type=F32)
    merged = g0_ref[...] * dot(ya_ref, wa_ref) + g1_ref[...] * dot(yb_ref, wb_ref) + g2_ref[...] * dot(yc_ref, wc_ref)
    o_ref[...] = merged.astype(BF16)


def _merge(ya, yb, yc, wa, wb, wc, g3):
    m = ya.shape[0]
    tn = 512
    y_spec = pl.BlockSpec((BLK, W_BR), lambda i, j: (i, 0))
    w_spec = pl.BlockSpec((W_BR, tn), lambda i, j: (0, j))
    gate = lambda k: pl.BlockSpec((None, BLK, tn), lambda i, j, k=k: (k, i, j))
    return pl.pallas_call(
        _merge_kernel,
        grid=(m // BLK, D_MODEL // tn),
        in_specs=[y_spec, y_spec, y_spec, w_spec, w_spec, w_spec, gate(0), gate(1), gate(2)],
        out_specs=pl.BlockSpec((BLK, tn), lambda i, j: (i, j)),
        out_shape=jax.ShapeDtypeStruct((m, D_MODEL), BF16),
        compiler_params=_params("parallel", "arbitrary"),
        name="merge",
    )(ya, yb, yc, wa, wb, wc, g3, g3, g3)


def _proj_residual_kernel(a_ref, w_ref, x_ref, o_ref):
    o_ref[...] = x_ref[...] + jnp.dot(a_ref[...], w_ref[...], preferred_element_type=F32)


def _proj_residual(a, w, x):
    m, k = a.shape
    tn = 1024
    return pl.pallas_call(
        _proj_residual_kernel,
        grid=(m // BLK, D_MODEL // tn),
        in_specs=[
            pl.BlockSpec((BLK, k), lambda i, j: (i, 0)),
            pl.BlockSpec((k, tn), lambda i, j: (0, j)),
            pl.BlockSpec((BLK, tn), lambda i, j: (i, j)),
        ],
        out_specs=pl.BlockSpec((BLK, tn), lambda i, j: (i, j)),
        out_shape=jax.ShapeDtypeStruct((m, D_MODEL), F32),
        compiler_params=_params("parallel", "arbitrary"),
        name="proj_residual",
    )(a, w, x)


def _ffn_kernel(x_ref, g_ref, wg_ref, wu_ref, wo_ref, o_ref, xn_ref):
    @pl.when(pl.program_id(1) == 0)
    def _():
        x = x_ref[...]
        xn = x * lax.rsqrt(jnp.mean(x * x, axis=-1, keepdims=True) + EPS) * g_ref[...]
        xn_ref[...] = xn.astype(BF16)
        o_ref[...] = x

    xn = xn_ref[...]
    gate = jnp.dot(xn, wg_ref[...], preferred_element_type=F32)
    up = jnp.dot(xn, wu_ref[...], preferred_element_type=F32)
    act = (jax.nn.silu(gate) * up).astype(BF16)
    o_ref[...] += jnp.dot(act, wo_ref[...], preferred_element_type=F32)


def _ffn(x, g, w_in, w_out):
    m = x.shape[0]
    tm, tf = 512, 512
    nf = D_FF // tf
    return pl.pallas_call(
        _ffn_kernel,
        grid=(m // tm, nf),
        in_specs=[
            pl.BlockSpec((tm, D_MODEL), lambda i, j: (i, 0)),
            pl.BlockSpec((1, D_MODEL), lambda i, j: (0, 0)),
            pl.BlockSpec((D_MODEL, tf), lambda i, j: (0, j)),
            pl.BlockSpec((D_MODEL, tf), lambda i, j: (0, j + nf)),
            pl.BlockSpec((tf, D_MODEL), lambda i, j: (j, 0)),
        ],
        out_specs=pl.BlockSpec((tm, D_MODEL), lambda i, j: (i, 0)),
        out_shape=jax.ShapeDtypeStruct((m, D_MODEL), F32),
        scratch_shapes=[pltpu.VMEM((tm, D_MODEL), BF16)],
        compiler_params=_params("parallel", "arbitrary"),
        name="ffn",
    )(x, g, w_in, w_in, w_out)


def _final_norm_kernel(x_ref, g_ref, o_ref):
    x = x_ref[...]
    o_ref[...] = x * lax.rsqrt(jnp.mean(x * x, axis=-1, keepdims=True) + EPS) * g_ref[...]


def _final_norm(x, g):
    m = x.shape[0]
    return pl.pallas_call(
        _final_norm_kernel,
        grid=(m // BLK,),
        in_specs=[pl.BlockSpec((BLK, D_MODEL), lambda i: (i, 0)), pl.BlockSpec((1, D_MODEL), lambda i: (0, 0))],
        out_specs=pl.BlockSpec((BLK, D_MODEL), lambda i: (i, 0)),
        out_shape=jax.ShapeDtypeStruct((m, D_MODEL), F32),
        compiler_params=_params("parallel"),
        name="final_norm",
    )(x, g)


def _layer(x, p, conv_bufs, s0, *, prompt):
    h3 = _norm_matmul_planes(x, p["g_mix"], p["w_in"], p["zero_b"], W_BR, sigmoid=False, name="in_proj")
    g3 = _norm_matmul_planes(x, p["g_mix"], p["w_gate"], p["b_gate"], D_MODEL, sigmoid=True, name="gates")
    ya, conv_new = _conv_branch(h3, p["conv_w"], conv_bufs, prompt=prompt)
    ys, *state = _s5_branch(h3, p["s5"], s0, prompt=prompt)
    yb = _glu(ys, p["w_glu"], p["b_glu"])
    yc, vn = _sgu_branch(h3, p["ln_v_g"], p["w_tril"], p["b_sgu"], prompt=prompt)
    merged = _merge(ya, yb, yc, p["w_conv_out"], p["w_ssm_out"], p["w_sgu_out"], g3)
    x = _proj_residual(merged, p["w_o"], x)
    x = _ffn(x, p["g_ffn"], p["w_ffn_in"], p["w_ffn_out"])
    return x, conv_new, state, vn


def kernel(x_prompt, x_sample, cache_conv, state_ssm_re, state_ssm_im, norm_mix_g, w_in, conv_w, w_conv_out, ssm_lam_re, ssm_lam_im, ssm_log_dt, ssm_b_re, ssm_b_im, ssm_c_re, ssm_c_im, ssm_d, w_glu, b_glu, w_ssm_out, ln_v_g, w_sgu_s, b_sgu_s, w_sgu_out, w_gate, b_gate, w_o, norm_ffn_g, w_ffn_in, w_ffn_out, norm_final_g):
    seq = x_prompt.shape[1]
    nb = seq // BLK
    xp = x_prompt.reshape(nb, SUBLANES, BLK // SUBLANES, D_MODEL).transpose(0, 2, 1, 3).reshape(seq, D_MODEL)
    xs = x_sample.transpose(1, 0, 2).reshape(BLK, D_MODEL)
    tril = jnp.tril(jnp.ones((128, 128), dtype=bool))

    conv_p, re_p, im_p, conv_s, re_s, im_s, v_s = [], [], [], [], [], [], []
    for l in range(DEPTH):
        p = {
            "g_mix": norm_mix_g[l].reshape(1, D_MODEL),
            "w_in": w_in[l].astype(BF16),
            "zero_b": jnp.zeros((1, 6 * W_BR), F32),
            "w_gate": w_gate[l].reshape(D_MODEL, 3 * D_MODEL).astype(BF16),
            "b_gate": b_gate[l].reshape(1, 3 * D_MODEL),
            "conv_w": conv_w[l],
            "s5": _s5_tables(ssm_lam_re[l], ssm_lam_im[l], ssm_log_dt[l], ssm_b_re[l], ssm_b_im[l],
                             ssm_c_re[l], ssm_c_im[l], ssm_d[l]),
            "w_glu": w_glu[l].astype(BF16),
            "b_glu": b_glu[l].reshape(1, W_BR),
            "ln_v_g": ln_v_g[l].reshape(1, W_BR),
            "w_tril": jnp.where(tril, w_sgu_s[l], 0).astype(BF16),
            "b_sgu": b_sgu_s[l],
            "w_conv_out": w_conv_out[l].astype(BF16),
            "w_ssm_out": w_ssm_out[l].astype(BF16),
            "w_sgu_out": w_sgu_out[l].astype(BF16),
            "w_o": w_o[l].astype(BF16),
            "g_ffn": norm_ffn_g[l].reshape(1, D_MODEL),
            "w_ffn_in": w_ffn_in[l].astype(BF16),
            "w_ffn_out": w_ffn_out[l].astype(BF16),
        }
        xp, cp, sp, _ = _layer(xp, p, None, None, prompt=True)
        bufs = (cache_conv[l, :, 0, :], cache_conv[l, :, 1, :])
        s0 = (state_ssm_re[l].reshape(32, N_Q * Q_ST), state_ssm_im[l].reshape(32, N_Q * Q_ST))
        xs, cs, ss, vs = _layer(xs, p, bufs, s0, prompt=False)
        conv_p.append(cp.reshape(1, 2, W_BR))
        re_p.append(sp[0][:, 0, :].reshape(1, 64, 64))
        im_p.append(sp[0][:, 1, :].reshape(1, 64, 64))
        conv_s.append(cs.reshape(2, 32, W_BR).transpose(1, 0, 2))
        re_s.append(ss[0].reshape(32, 64, 64))
        im_s.append(ss[1].reshape(32, 64, 64))
        v_s.append(vs.reshape(32, 32, W_BR).transpose(1, 0, 2))

    g_fin = norm_final_g.reshape(1, D_MODEL)
    yp = _final_norm(xp, g_fin)
    ys = _final_norm(xs, g_fin)
    y_prompt = yp.reshape(nb, BLK // SUBLANES, SUBLANES, D_MODEL).transpose(0, 2, 1, 3).reshape(1, seq, D_MODEL)
    y_sample = ys.reshape(32, 32, D_MODEL).transpose(1, 0, 2)
    return (y_prompt, y_sample, jnp.stack(conv_p), jnp.stack(re_p), jnp.stack(im_p),
            jnp.stack(conv_s), jnp.stack(re_s), jnp.stack(im_s), jnp.stack(v_s))
```

```python
import functools

import jax
import jax.numpy as jnp
from jax import lax
from jax.experimental import pallas as pl
from jax.experimental.pallas import tpu as pltpu

F32 = jnp.float32
BF16 = jnp.bfloat16

D_MODEL = 2048
DEPTH = 4
W_BR = 1024
SSM_GROUPS = 64
N_Q = 4
Q_GROUPS = SSM_GROUPS // N_Q
Q_IN = Q_GROUPS * 16
Q_ST = Q_GROUPS * 64
SGU_HEADS = 8
HEAD_DIM = 128
D_FF = 5632
EPS = 1e-6
BLK = 1024
SUBLANES = 8
CONV_TC = 256
VMEM_LIMIT_BYTES = 56 * 1024 * 1024


def _params(*sem):
    return pltpu.CompilerParams(dimension_semantics=sem, vmem_limit_bytes=VMEM_LIMIT_BYTES)


def _gelu(x):
    return jax.nn.gelu(x, approximate=True)


def _rms(x, g):
    return x * lax.rsqrt(jnp.mean(x * x, axis=-1, keepdims=True) + EPS) * g


def _in_proj_kernel(x_ref, g_ref, w_ref, lng_ref, h_ref, xn_ref):
    j = pl.program_id(1)

    @pl.when(j == 0)
    def _():
        xn_ref[...] = _rms(x_ref[...], g_ref[...]).astype(BF16)

    acc = jnp.dot(xn_ref[...], w_ref[...], preferred_element_type=F32)

    @pl.when(j == 0)
    def _():
        h_ref[...] = acc

    @pl.when(j == 1)
    def _():
        h_ref[...] = _gelu(acc)

    @pl.when(j == 2)
    def _():
        gv = _gelu(acc)
        xc = gv - jnp.mean(gv, axis=-1, keepdims=True)
        h_ref[...] = xc * lax.rsqrt(jnp.mean(xc * xc, axis=-1, keepdims=True) + EPS) * lng_ref[...]


def _in_proj(x, g, w, ln_g):
    m = x.shape[0]
    tm = 512
    return pl.pallas_call(
        _in_proj_kernel,
        grid=(m // tm, 3),
        in_specs=[
            pl.BlockSpec((tm, D_MODEL), lambda i, j: (i, 0)),
            pl.BlockSpec((1, D_MODEL), lambda i, j: (0, 0)),
            pl.BlockSpec((D_MODEL, W_BR), lambda i, j: (0, j)),
            pl.BlockSpec((1, W_BR), lambda i, j: (0, 0)),
        ],
        out_specs=[
            pl.BlockSpec((None, tm, W_BR), lambda i, j: (j, i, 0)),
            pl.BlockSpec((tm, D_MODEL), lambda i, j: (i, 0)),
        ],
        out_shape=[jax.ShapeDtypeStruct((3, m, W_BR), F32), jax.ShapeDtypeStruct((m, D_MODEL), BF16)],
        compiler_params=_params("parallel", "arbitrary"),
        name="in_proj",
    )(x, g, w, ln_g)


def _conv_taps(z, b1, b2, w, r):
    prev1 = jnp.concatenate([b1, z[:BLK - r]], axis=0)
    prev2 = jnp.concatenate([b2, b1, z[:BLK - 2 * r]], axis=0)
    return w[0:1] * prev2 + w[1:2] * prev1 + w[2:3] * z


def _conv_split(xn_ref, w_ref):
    acc = jnp.dot(xn_ref[...], w_ref[...], preferred_element_type=F32)
    b = acc[:, :CONV_TC]
    z = acc[:, CONV_TC:2 * CONV_TC] * acc[:, 2 * CONV_TC:]
    return b, z


def _conv_prompt_kernel(xn_ref, w_ref, cw_ref, ya_ref, cn_ref, carry_ref):
    r = SUBLANES
    j = pl.program_id(1)

    @pl.when(pl.program_id(0) == 0)
    def _():
        carry_ref[j] = jnp.zeros(carry_ref.shape[1:], F32)

    b, z = _conv_split(xn_ref, w_ref)
    last1 = z[BLK - r:]
    last2 = z[BLK - 2 * r:BLK - r]
    row = lax.broadcasted_iota(jnp.int32, last1.shape, 0)
    carry = carry_ref[j]
    b1 = jnp.where(row == 0, carry[1:2, :], pltpu.roll(last1, 1, 0))
    b2 = jnp.where(row == 0, carry[0:1, :], pltpu.roll(last2, 1, 0))
    ya_ref[...] = (b * _conv_taps(z, b1, b2, cw_ref[...], r)).astype(BF16)
    tail = jnp.where(row == 0, last2[r - 1:r], last1[r - 1:r])
    carry_ref[j] = tail
    cn_ref[...] = tail


def _conv_sample_kernel(xn_ref, w_ref, cw_ref, buf0_ref, buf1_ref, ya_ref, cn_ref):
    r = 32
    b, z = _conv_split(xn_ref, w_ref)
    ya_ref[...] = (b * _conv_taps(z, buf1_ref[...], buf0_ref[...], cw_ref[...], r)).astype(BF16)
    cn_ref[...] = z[BLK - 2 * r:]


def _conv_branch(xn, w, conv_w, bufs, *, prompt):
    m = xn.shape[0]
    tc = CONV_TC
    nt = W_BR // tc
    in_specs = [
        pl.BlockSpec((BLK, D_MODEL), lambda i, j: (i, 0)),
        pl.BlockSpec((D_MODEL, 3 * tc), lambda i, j: (0, j)),
        pl.BlockSpec((3, tc), lambda i, j: (0, j)),
    ]
    ya_spec = pl.BlockSpec((BLK, tc), lambda i, j: (i, j))
    if prompt:
        return pl.pallas_call(
            _conv_prompt_kernel,
            grid=(m // BLK, nt),
            in_specs=in_specs,
            out_specs=[ya_spec, pl.BlockSpec((None, SUBLANES, tc), lambda i, j: (i, 0, j))],
            out_shape=[jax.ShapeDtypeStruct((m, W_BR), BF16),
                       jax.ShapeDtypeStruct((m // BLK, SUBLANES, W_BR), F32)],
            scratch_shapes=[pltpu.VMEM((nt, SUBLANES, tc), F32)],
            compiler_params=_params("arbitrary", "arbitrary"),
            name="conv_prompt",
        )(xn, w, conv_w)
    buf_spec = pl.BlockSpec((32, tc), lambda i, j: (0, j))
    return pl.pallas_call(
        _conv_sample_kernel,
        grid=(1, nt),
        in_specs=in_specs + [buf_spec, buf_spec],
        out_specs=[ya_spec, pl.BlockSpec((64, tc), lambda i, j: (0, j))],
        out_shape=[jax.ShapeDtypeStruct((m, W_BR), BF16), jax.ShapeDtypeStruct((64, W_BR), F32)],
        compiler_params=_params("arbitrary", "arbitrary"),
        name="conv_sample",
    )(xn, w, conv_w, bufs[0], bufs[1])


def _conv_weight(w_in_l):
    nt = W_BR // CONV_TC
    w = w_in_l[:, :3 * W_BR].reshape(D_MODEL, 3, nt, CONV_TC).transpose(0, 2, 1, 3)
    return w.reshape(D_MODEL, 3 * W_BR).astype(BF16)


def _scan_rows(hre, him, lr, li, init, base, stride, steps):
    def step(j, st):
        sr, si = st
        r0 = pl.multiple_of(base + j * stride, SUBLANES)
        nr = lr * sr - li * si + hre[pl.ds(r0, SUBLANES), :]
        ni = lr * si + li * sr + him[pl.ds(r0, SUBLANES), :]
        hre[pl.ds(r0, SUBLANES), :] = nr
        him[pl.ds(r0, SUBLANES), :] = ni
        return nr, ni

    return lax.fori_loop(0, steps, step, init, unroll=4)


def _s5_project_in(u_ref, wb_ref, hre, him):
    for rows in (slice(0, BLK // 2), slice(BLK // 2, BLK)):
        bu = jnp.dot(u_ref[rows, :].astype(BF16), wb_ref[...], preferred_element_type=F32)
        hre[rows, :] = bu[:, :Q_ST]
        him[rows, :] = bu[:, Q_ST:]


def _s5_project_out(u_ref, wc_ref, d_ref, y_ref, hre, him):
    for rows in (slice(0, BLK // 2), slice(BLK // 2, BLK)):
        h = jnp.concatenate([hre[rows, :], him[rows, :]], axis=1).astype(BF16)
        y_ref[rows, :] = jnp.dot(h, wc_ref[...], preferred_element_type=F32) + d_ref[...] * u_ref[rows, :]


def _s5_prompt_kernel(u_ref, wb_ref, lam_ref, lams_ref, wc_ref, d_ref, y_ref, sfin_ref, hre, him, carry):
    @pl.when(pl.program_id(1) == 0)
    def _():
        carry[...] = jnp.zeros_like(carry)

    _s5_project_in(u_ref, wb_ref, hre, him)
    tile = (SUBLANES, Q_ST)
    lr = jnp.broadcast_to(lam_ref[0:1, :], tile)
    li = jnp.broadcast_to(lam_ref[1:2, :], tile)
    zero = jnp.zeros(tile, F32)
    steps = BLK // SUBLANES
    er, ei = _scan_rows(hre, him, lr, li, (zero, zero), 0, SUBLANES, steps)
    pr, pi = lams_ref[0:1, :], lams_ref[1:2, :]
    cr, ci = carry[0:1, :], carry[1:2, :]
    row = lax.broadcasted_iota(jnp.int32, tile, 0)
    sin_r, sin_i = zero, zero
    for s in range(SUBLANES):
        sin_r = jnp.where(row == s, cr, sin_r)
        sin_i = jnp.where(row == s, ci, sin_i)
        cr, ci = er[s:s + 1] + pr * cr - pi * ci, ei[s:s + 1] + pr * ci + pi * cr
    carry[0:1, :] = cr
    carry[1:2, :] = ci
    sfin_ref[0:1, :] = cr
    sfin_ref[1:2, :] = ci

    def fix(j, e):
        fr, fi = e
        nr = lr * fr - li * fi
        ni = lr * fi + li * fr
        r0 = pl.multiple_of(j * SUBLANES, SUBLANES)
        hre[pl.ds(r0, SUBLANES), :] += nr
        him[pl.ds(r0, SUBLANES), :] += ni
        return nr, ni

    lax.fori_loop(0, steps, fix, (sin_r, sin_i), unroll=4)
    _s5_project_out(u_ref, wc_ref, d_ref, y_ref, hre, him)


def _s5_sample_kernel(u_ref, wb_ref, lam_ref, wc_ref, d_ref, s0r_ref, s0i_ref, y_ref, sr_ref, si_ref, hre, him):
    _s5_project_in(u_ref, wb_ref, hre, him)
    tile = (SUBLANES, Q_ST)
    lr = jnp.broadcast_to(lam_ref[0:1, :], tile)
    li = jnp.broadcast_to(lam_ref[1:2, :], tile)
    for r in range(32 // SUBLANES):
        rows = slice(r * SUBLANES, (r + 1) * SUBLANES)
        er, ei = _scan_rows(hre, him, lr, li, (s0r_ref[rows, :], s0i_ref[rows, :]), r * SUBLANES, 32, 32)
        sr_ref[rows, :] = er
        si_ref[rows, :] = ei
    _s5_project_out(u_ref, wc_ref, d_ref, y_ref, hre, him)


def _s5_branch(h3, tabs, s0, *, prompt):
    m = h3.shape[1]
    u_spec = pl.BlockSpec((None, BLK, Q_IN), lambda q, i: (0, i, q))
    wb_spec = pl.BlockSpec((None, Q_IN, 2 * Q_ST), lambda q, i: (q, 0, 0))
    lam_spec = pl.BlockSpec((None, 2, Q_ST), lambda q, i: (q, 0, 0))
    wc_spec = pl.BlockSpec((None, 2 * Q_ST, Q_IN), lambda q, i: (q, 0, 0))
    d_spec = pl.BlockSpec((1, Q_IN), lambda q, i: (0, q))
    y_spec = pl.BlockSpec((BLK, Q_IN), lambda q, i: (i, q))
    scratch = [pltpu.VMEM((BLK, Q_ST), F32), pltpu.VMEM((BLK, Q_ST), F32)]
    if prompt:
        return pl.pallas_call(
            _s5_prompt_kernel,
            grid=(N_Q, m // BLK),
            in_specs=[u_spec, wb_spec, lam_spec, lam_spec, wc_spec, d_spec],
            out_specs=[y_spec, pl.BlockSpec((None, 2, Q_ST), lambda q, i: (q, 0, 0))],
            out_shape=[jax.ShapeDtypeStruct((m, W_BR), F32), jax.ShapeDtypeStruct((N_Q, 2, Q_ST), F32)],
            scratch_shapes=scratch + [pltpu.VMEM((SUBLANES, Q_ST), F32)],
            compiler_params=_params("parallel", "arbitrary"),
            name="s5_prompt",
        )(h3, tabs["wb"], tabs["lam"], tabs["lam_seg"], tabs["wc"], tabs["d"])
    st_spec = pl.BlockSpec((32, Q_ST), lambda q, i: (0, q))
    return pl.pallas_call(
        _s5_sample_kernel,
        grid=(N_Q, 1),
        in_specs=[u_spec, wb_spec, lam_spec, wc_spec, d_spec, st_spec, st_spec],
        out_specs=[y_spec, st_spec, st_spec],
        out_shape=[jax.ShapeDtypeStruct((m, W_BR), F32)] + [jax.ShapeDtypeStruct((32, N_Q * Q_ST), F32)] * 2,
        scratch_shapes=scratch,
        compiler_params=_params("parallel", "arbitrary"),
        name="s5_sample",
    )(h3, tabs["wb"], tabs["lam"], tabs["wc"], tabs["d"], s0[0], s0[1])


def _s5_tables(lam_re, lam_im, log_dt, b_re, b_im, c_re, c_im, d):
    dt = jnp.exp(log_dt)[:, None]
    mag = jnp.exp(lam_re * dt)
    lbr, lbi = mag * jnp.cos(lam_im * dt), mag * jnp.sin(lam_im * dt)
    den = lam_re * lam_re + lam_im * lam_im
    qr = ((lbr - 1.0) * lam_re + lbi * lam_im) / den
    qi = (lbi * lam_re - (lbr - 1.0) * lam_im) / den
    bbr = qr[:, :, None] * b_re - qi[:, :, None] * b_im
    bbi = qr[:, :, None] * b_im + qi[:, :, None] * b_re
    eye = jnp.eye(Q_GROUPS, dtype=F32)

    def block_in(b):
        return jnp.einsum("ab,qapi->qaibp", eye, b.reshape(N_Q, Q_GROUPS, 64, 16)).reshape(N_Q, Q_IN, Q_ST)

    def block_out(c):
        return jnp.einsum("ab,qaip->qapbi", eye, c.reshape(N_Q, Q_GROUPS, 16, 64)).reshape(N_Q, Q_ST, Q_IN)

    wb = jnp.concatenate([block_in(bbr), block_in(bbi)], axis=2).astype(BF16)
    wc = jnp.concatenate([block_out(c_re), -block_out(c_im)], axis=1).astype(BF16)
    sr, si = lbr, lbi
    for _ in range(7):
        sr, si = sr * sr - si * si, 2.0 * sr * si
    pack = lambda a, b: jnp.stack([a.reshape(N_Q, Q_ST), b.reshape(N_Q, Q_ST)], axis=1)
    return {"wb": wb, "wc": wc, "lam": pack(lbr, lbi), "lam_seg": pack(sr, si), "d": d.reshape(1, W_BR)}


def _sgu_kernel(k_ref, v_ref, u_ref, bs_ref, yc_ref):
    for blk in range(v_ref.shape[0] // BLK):
        rows = slice(blk * BLK, (blk + 1) * BLK)
        mix = jnp.dot(k_ref[...], v_ref[rows, :].astype(BF16), preferred_element_type=F32) + bs_ref[...]
        yc_ref[rows, :] = (u_ref[rows, :] * mix).astype(BF16)


def _sgu_branch(h3, kmat, bias):
    m = h3.shape[1]
    tm = min(m, 2 * BLK)
    plane = lambda k: pl.BlockSpec((None, tm, HEAD_DIM), lambda h, i, k=k: (k, i, h))
    return pl.pallas_call(
        _sgu_kernel,
        grid=(SGU_HEADS, m // tm),
        in_specs=[
            pl.BlockSpec((None, BLK, BLK), lambda h, i: (h, 0, 0)),
            plane(2),
            plane(1),
            pl.BlockSpec((None, BLK, 1), lambda h, i: (h, 0, 0)),
        ],
        out_specs=pl.BlockSpec((tm, HEAD_DIM), lambda h, i: (i, h)),
        out_shape=jax.ShapeDtypeStruct((m, W_BR), BF16),
        compiler_params=_params("parallel", "arbitrary"),
        name="sgu",
    )(kmat, h3, h3, bias)


def _sgu_tables(w_s, b_s, steps, width):
    w = jnp.where(jnp.tril(jnp.ones((128, 128), dtype=bool)), w_s, 0)[:, :steps, :steps]
    eye = jnp.eye(width, dtype=F32)
    kmat = jnp.einsum("hjk,st->hjskt", w, eye).reshape(SGU_HEADS, BLK, BLK).astype(BF16)
    bias = jnp.repeat(b_s[:, :steps], width, axis=1)[:, :, None]
    return kmat, bias


def _gates_kernel(xn_ref, w_ref, b_ref, o_ref):
    acc = jnp.dot(xn_ref[...], w_ref[...], preferred_element_type=F32) + b_ref[...]
    o_ref[...] = jax.nn.sigmoid(acc).astype(BF16)


def _gates(xn, w, b):
    m = xn.shape[0]
    tn = 1024
    per = D_MODEL // tn
    return pl.pallas_call(
        _gates_kernel,
        grid=(m // BLK, 3 * per),
        in_specs=[
            pl.BlockSpec((BLK, D_MODEL), lambda i, j: (i, 0)),
            pl.BlockSpec((D_MODEL, tn), lambda i, j: (0, j)),
            pl.BlockSpec((1, tn), lambda i, j: (0, j)),
        ],
        out_specs=pl.BlockSpec((None, BLK, tn), lambda i, j: (j // per, i, j % per)),
        out_shape=jax.ShapeDtypeStruct((3, m, D_MODEL), BF16),
        compiler_params=_params("parallel", "arbitrary"),
        name="gates",
    )(xn, w, b)


def _glu_kernel(ys_ref, w_ref, b_ref, o_ref):
    g = _gelu(ys_ref[...])
    gate = jax.nn.sigmoid(jnp.dot(g.astype(BF16), w_ref[...], preferred_element_type=F32) + b_ref[...])
    o_ref[...] = (g * gate).astype(BF16)


def _glu(ys, w, b):
    m = ys.shape[0]
    return pl.pallas_call(
        _glu_kernel,
        grid=(m // BLK,),
        in_specs=[
            pl.BlockSpec((BLK, W_BR), lambda i: (i, 0)),
            pl.BlockSpec((W_BR, W_BR), lambda i: (0, 0)),
            pl.BlockSpec((1, W_BR), lambda i: (0, 0)),
        ],
        out_specs=pl.BlockSpec((BLK, W_BR), lambda i: (i, 0)),
        out_shape=jax.ShapeDtypeStruct((m, W_BR), BF16),
        compiler_params=_params("parallel"),
        name="glu",
    )(ys, w, b)


MERGE_TN = 512
MERGE_NT = D_MODEL // MERGE_TN


def _merge_project_kernel(ya_ref, yb_ref, yc_ref, wa_ref, wb_ref, wc_ref, g0_ref, g1_ref, g2_ref,
                          wo_ref, x_ref, o_ref, merged_ref):
    j = pl.program_id(1)

    @pl.when(j < MERGE_NT)
    def _():
        dot = lambda a, w: jnp.dot(a[...], w[...], preferred_element_type=F32)
        merged = (g0_ref[...] * dot(ya_ref, wa_ref) + g1_ref[...] * dot(yb_ref, wb_ref)
                  + g2_ref[...] * dot(yc_ref, wc_ref))
        merged_ref[j] = merged.astype(BF16)

    @pl.when(j >= MERGE_NT)
    def _():
        acc = x_ref[...]
        for k in range(MERGE_NT):
            acc += jnp.dot(merged_ref[k], wo_ref[k * MERGE_TN:(k + 1) * MERGE_TN, :], preferred_element_type=F32)
        o_ref[...] = acc


def _merge_project(ya, yb, yc, wa, wb, wc, g3, wo, x):
    m = ya.shape[0]
    tn, nt = MERGE_TN, MERGE_NT
    first = lambda j: jnp.minimum(j, nt - 1)
    second = lambda j: jnp.maximum(j - nt, 0)
    y_spec = pl.BlockSpec((BLK, W_BR), lambda i, j: (i, 0))
    w_spec = pl.BlockSpec((W_BR, tn), lambda i, j: (0, first(j)))
    gate = lambda k: pl.BlockSpec((None, BLK, tn), lambda i, j, k=k: (k, i, first(j)))
    return pl.pallas_call(
        _merge_project_kernel,
        grid=(m // BLK, 2 * nt),
        in_specs=[y_spec, y_spec, y_spec, w_spec, w_spec, w_spec, gate(0), gate(1), gate(2),
                  pl.BlockSpec((D_MODEL, tn), lambda i, j: (0, second(j))),
                  pl.BlockSpec((BLK, tn), lambda i, j: (i, second(j)))],
        out_specs=pl.BlockSpec((BLK, tn), lambda i, j: (i, second(j))),
        out_shape=jax.ShapeDtypeStruct((m, D_MODEL), F32),
        scratch_shapes=[pltpu.VMEM((nt, BLK, tn), BF16)],
        compiler_params=_params("parallel", "arbitrary"),
        name="merge_project",
    )(ya, yb, yc, wa, wb, wc, g3, g3, g3, wo, x)


def _ffn_kernel(x_ref, g_ref, wg_ref, wu_ref, wo_ref, o_ref, xn_ref):
    @pl.when(pl.program_id(1) == 0)
    def _():
        x = x_ref[...]
        xn_ref[...] = _rms(x, g_ref[...]).astype(BF16)
        o_ref[...] = x

    xn = xn_ref[...]
    gate = jnp.dot(xn, wg_ref[...], preferred_element_type=F32)
    up = jnp.dot(xn, wu_ref[...], preferred_element_type=F32)
    act = (jax.nn.silu(gate) * up).astype(BF16)
    o_ref[...] += jnp.dot(act, wo_ref[...], preferred_element_type=F32)


def _ffn(x, g, w_in, w_out):
    m = x.shape[0]
    tm, tf = 512, 512
    nf = D_FF // tf
    return pl.pallas_call(
        _ffn_kernel,
        grid=(m // tm, nf),
        in_specs=[
            pl.BlockSpec((tm, D_MODEL), lambda i, j: (i, 0)),
            pl.BlockSpec((1, D_MODEL), lambda i, j: (0, 0)),
            pl.BlockSpec((D_MODEL, tf), lambda i, j: (0, j)),
            pl.BlockSpec((D_MODEL, tf), lambda i, j: (0, j + nf)),
            pl.BlockSpec((tf, D_MODEL), lambda i, j: (j, 0)),
        ],
        out_specs=pl.BlockSpec((tm, D_MODEL), lambda i, j: (i, 0)),
        out_shape=jax.ShapeDtypeStruct((m, D_MODEL), F32),
        scratch_shapes=[pltpu.VMEM((tm, D_MODEL), BF16)],
        compiler_params=_params("parallel", "arbitrary"),
        name="ffn",
    )(x, g, w_in, w_in, w_out)


def _final_norm_kernel(x_ref, g_ref, o_ref):
    o_ref[...] = _rms(x_ref[...], g_ref[...])


def _final_norm(x, g):
    m = x.shape[0]
    return pl.pallas_call(
        _final_norm_kernel,
        grid=(m // BLK,),
        in_specs=[pl.BlockSpec((BLK, D_MODEL), lambda i: (i, 0)), pl.BlockSpec((1, D_MODEL), lambda i: (0, 0))],
        out_specs=pl.BlockSpec((BLK, D_MODEL), lambda i: (i, 0)),
        out_shape=jax.ShapeDtypeStruct((m, D_MODEL), F32),
        compiler_params=_params("parallel"),
        name="final_norm",
    )(x, g)


def _layer(x, p, sgu_tabs, conv_bufs, s0, *, prompt):
    h3, xn = _in_proj(x, p["g_mix"], p["w_in_bc"], p["ln_v_g"])
    g3 = _gates(xn, p["w_gate"], p["b_gate"])
    ya, conv_new = _conv_branch(xn, p["w_in_a"], p["conv_w"], conv_bufs, prompt=prompt)
    ys, *state = _s5_branch(h3, p["s5"], s0, prompt=prompt)
    yb = _glu(ys, p["w_glu"], p["b_glu"])
    yc = _sgu_branch(h3, *sgu_tabs)
    x = _merge_project(ya, yb, yc, p["w_conv_out"], p["w_ssm_out"], p["w_sgu_out"], g3, p["w_o"], x)
    x = _ffn(x, p["g_ffn"], p["w_ffn_in"], p["w_ffn_out"])
    return x, conv_new, state, h3[2]


def kernel(x_prompt, x_sample, cache_conv, state_ssm_re, state_ssm_im, norm_mix_g, w_in, conv_w, w_conv_out, ssm_lam_re, ssm_lam_im, ssm_log_dt, ssm_b_re, ssm_b_im, ssm_c_re, ssm_c_im, ssm_d, w_glu, b_glu, w_ssm_out, ln_v_g, w_sgu_s, b_sgu_s, w_sgu_out, w_gate, b_gate, w_o, norm_ffn_g, w_ffn_in, w_ffn_out, norm_final_g):
    seq = x_prompt.shape[1]
    nb = seq // BLK
    xp = x_prompt.reshape(nb, SUBLANES, BLK // SUBLANES, D_MODEL).transpose(0, 2, 1, 3).reshape(seq, D_MODEL)
    xs = x_sample.transpose(1, 0, 2).reshape(BLK, D_MODEL)

    conv_p, re_p, im_p, conv_s, re_s, im_s, v_s = [], [], [], [], [], [], []
    for l in range(DEPTH):
        p = {
            "g_mix": norm_mix_g[l].reshape(1, D_MODEL),
            "w_in_a": _conv_weight(w_in[l]),
            "w_in_bc": w_in[l][:, 3 * W_BR:].astype(BF16),
            "w_gate": w_gate[l].reshape(D_MODEL, 3 * D_MODEL).astype(BF16),
            "b_gate": b_gate[l].reshape(1, 3 * D_MODEL),
            "conv_w": conv_w[l],
            "s5": _s5_tables(ssm_lam_re[l], ssm_lam_im[l], ssm_log_dt[l], ssm_b_re[l], ssm_b_im[l],
                             ssm_c_re[l], ssm_c_im[l], ssm_d[l]),
            "w_glu": w_glu[l].astype(BF16),
            "b_glu": b_glu[l].reshape(1, W_BR),
            "ln_v_g": ln_v_g[l].reshape(1, W_BR),
            "w_conv_out": w_conv_out[l].astype(BF16),
            "w_ssm_out": w_ssm_out[l].astype(BF16),
            "w_sgu_out": w_sgu_out[l].astype(BF16),
            "w_o": w_o[l].astype(BF16),
            "g_ffn": norm_ffn_g[l].reshape(1, D_MODEL),
            "w_ffn_in": w_ffn_in[l].astype(BF16),
            "w_ffn_out": w_ffn_out[l].astype(BF16),
        }
        sgu_p = _sgu_tables(w_sgu_s[l], b_sgu_s[l], BLK // SUBLANES, SUBLANES)
        sgu_s = _sgu_tables(w_sgu_s[l], b_sgu_s[l], 32, 32)
        xp, cp, sp, _ = _layer(xp, p, sgu_p, None, None, prompt=True)
        bufs = (cache_conv[l, :, 0, :], cache_conv[l, :, 1, :])
        s0 = (state_ssm_re[l].reshape(32, N_Q * Q_ST), state_ssm_im[l].reshape(32, N_Q * Q_ST))
        xs, cs, ss, vs = _layer(xs, p, sgu_s, bufs, s0, prompt=False)
        conv_p.append(cp[-1, 0:2].reshape(1, 2, W_BR))
        re_p.append(sp[0][:, 0, :].reshape(1, 64, 64))
        im_p.append(sp[0][:, 1, :].reshape(1, 64, 64))
        conv_s.append(cs.reshape(2, 32, W_BR).transpose(1, 0, 2))
        re_s.append(ss[0].reshape(32, 64, 64))
        im_s.append(ss[1].reshape(32, 64, 64))
        v_s.append(vs.reshape(32, 32, W_BR).transpose(1, 0, 2))

    g_fin = norm_final_g.reshape(1, D_MODEL)
    yp = _final_norm(xp, g_fin)
    ys = _final_norm(xs, g_fin)
    y_prompt = yp.reshape(nb, BLK // SUBLANES, SUBLANES, D_MODEL).transpose(0, 2, 1, 3).reshape(1, seq, D_MODEL)
    y_sample = ys.reshape(32, 32, D_MODEL).transpose(1, 0, 2)
    return (y_prompt, y_sample, jnp.stack(conv_p), jnp.stack(re_p), jnp.stack(im_p),
            jnp.stack(conv_s), jnp.stack(re_s), jnp.stack(im_s), jnp.stack(v_s))
```

```python
import functools

import jax
import jax.numpy as jnp
from jax import lax
from jax.experimental import pallas as pl
from jax.experimental.pallas import tpu as pltpu

F32 = jnp.float32
BF16 = jnp.bfloat16

D_MODEL = 2048
DEPTH = 4
W_BR = 1024
SSM_GROUPS = 64
N_Q = 4
Q_GROUPS = SSM_GROUPS // N_Q
Q_IN = Q_GROUPS * 16
Q_ST = Q_GROUPS * 64
SGU_HEADS = 8
HEAD_DIM = 128
D_FF = 5632
EPS = 1e-6
BLK = 1024
SUBLANES = 8
CONV_TC = 256
VMEM_LIMIT_BYTES = 56 * 1024 * 1024


def _params(*sem):
    return pltpu.CompilerParams(dimension_semantics=sem, vmem_limit_bytes=VMEM_LIMIT_BYTES)


def _gelu(x):
    return jax.nn.gelu(x, approximate=True)


def _rms(x, g):
    return x * lax.rsqrt(jnp.mean(x * x, axis=-1, keepdims=True) + EPS) * g


def _in_proj_kernel(x_ref, g_ref, w_ref, lng_ref, h_ref, xn_ref):
    j = pl.program_id(1)

    @pl.when(j == 0)
    def _():
        xn_ref[...] = _rms(x_ref[...], g_ref[...]).astype(BF16)

    acc = jnp.dot(xn_ref[...], w_ref[...], preferred_element_type=F32)

    @pl.when(j == 0)
    def _():
        h_ref[...] = acc

    @pl.when(j == 1)
    def _():
        h_ref[...] = _gelu(acc)

    @pl.when(j == 2)
    def _():
        gv = _gelu(acc)
        xc = gv - jnp.mean(gv, axis=-1, keepdims=True)
        h_ref[...] = xc * lax.rsqrt(jnp.mean(xc * xc, axis=-1, keepdims=True) + EPS) * lng_ref[...]


def _in_proj(x, g, w, ln_g):
    m = x.shape[0]
    tm = 512
    return pl.pallas_call(
        _in_proj_kernel,
        grid=(m // tm, 3),
        in_specs=[
            pl.BlockSpec((tm, D_MODEL), lambda i, j: (i, 0)),
            pl.BlockSpec((1, D_MODEL), lambda i, j: (0, 0)),
            pl.BlockSpec((D_MODEL, W_BR), lambda i, j: (0, j)),
            pl.BlockSpec((1, W_BR), lambda i, j: (0, 0)),
        ],
        out_specs=[
            pl.BlockSpec((None, tm, W_BR), lambda i, j: (j, i, 0)),
            pl.BlockSpec((tm, D_MODEL), lambda i, j: (i, 0)),
        ],
        out_shape=[jax.ShapeDtypeStruct((3, m, W_BR), F32), jax.ShapeDtypeStruct((m, D_MODEL), BF16)],
        compiler_params=_params("parallel", "arbitrary"),
        name="in_proj",
    )(x, g, w, ln_g)


def _conv_taps(z, b1, b2, w, r):
    prev1 = jnp.concatenate([b1, z[:BLK - r]], axis=0)
    prev2 = jnp.concatenate([b2, b1, z[:BLK - 2 * r]], axis=0)
    return w[0:1] * prev2 + w[1:2] * prev1 + w[2:3] * z


def _conv_split(xn_ref, w_ref):
    acc = jnp.dot(xn_ref[...], w_ref[...], preferred_element_type=F32)
    b = acc[:, :CONV_TC]
    z = acc[:, CONV_TC:2 * CONV_TC] * acc[:, 2 * CONV_TC:]
    return b, z


def _conv_prompt_kernel(xn_ref, w_ref, cw_ref, ya_ref, cn_ref, carry_ref):
    r = SUBLANES
    j = pl.program_id(1)

    @pl.when(pl.program_id(0) == 0)
    def _():
        carry_ref[j] = jnp.zeros(carry_ref.shape[1:], F32)

    b, z = _conv_split(xn_ref, w_ref)
    last1 = z[BLK - r:]
    last2 = z[BLK - 2 * r:BLK - r]
    row = lax.broadcasted_iota(jnp.int32, last1.shape, 0)
    carry = carry_ref[j]
    b1 = jnp.where(row == 0, carry[1:2, :], pltpu.roll(last1, 1, 0))
    b2 = jnp.where(row == 0, carry[0:1, :], pltpu.roll(last2, 1, 0))
    ya_ref[...] = (b * _conv_taps(z, b1, b2, cw_ref[...], r)).astype(BF16)
    tail = jnp.where(row == 0, last2[r - 1:r], last1[r - 1:r])
    carry_ref[j] = tail
    cn_ref[...] = tail


def _conv_sample_kernel(xn_ref, w_ref, cw_ref, buf0_ref, buf1_ref, ya_ref, cn_ref):
    r = 32
    b, z = _conv_split(xn_ref, w_ref)
    ya_ref[...] = (b * _conv_taps(z, buf1_ref[...], buf0_ref[...], cw_ref[...], r)).astype(BF16)
    cn_ref[...] = z[BLK - 2 * r:]


def _conv_branch(xn, w, conv_w, bufs, *, prompt):
    m = xn.shape[0]
    tc = CONV_TC
    nt = W_BR // tc
    in_specs = [
        pl.BlockSpec((BLK, D_MODEL), lambda i, j: (i, 0)),
        pl.BlockSpec((D_MODEL, 3 * tc), lambda i, j: (0, j)),
        pl.BlockSpec((3, tc), lambda i, j: (0, j)),
    ]
    ya_spec = pl.BlockSpec((BLK, tc), lambda i, j: (i, j))
    if prompt:
        return pl.pallas_call(
            _conv_prompt_kernel,
            grid=(m // BLK, nt),
            in_specs=in_specs,
            out_specs=[ya_spec, pl.BlockSpec((None, SUBLANES, tc), lambda i, j: (i, 0, j))],
            out_shape=[jax.ShapeDtypeStruct((m, W_BR), BF16),
                       jax.ShapeDtypeStruct((m // BLK, SUBLANES, W_BR), F32)],
            scratch_shapes=[pltpu.VMEM((nt, SUBLANES, tc), F32)],
            compiler_params=_params("arbitrary", "arbitrary"),
            name="conv_prompt",
        )(xn, w, conv_w)
    buf_spec = pl.BlockSpec((32, tc), lambda i, j: (0, j))
    return pl.pallas_call(
        _conv_sample_kernel,
        grid=(1, nt),
        in_specs=in_specs + [buf_spec, buf_spec],
        out_specs=[ya_spec, pl.BlockSpec((64, tc), lambda i, j: (0, j))],
        out_shape=[jax.ShapeDtypeStruct((m, W_BR), BF16), jax.ShapeDtypeStruct((64, W_BR), F32)],
        compiler_params=_params("arbitrary", "arbitrary"),
        name="conv_sample",
    )(xn, w, conv_w, bufs[0], bufs[1])


def _conv_weight(w_in_l):
    nt = W_BR // CONV_TC
    w = w_in_l[:, :3 * W_BR].reshape(D_MODEL, 3, nt, CONV_TC).transpose(0, 2, 1, 3)
    return w.reshape(D_MODEL, 3 * W_BR).astype(BF16)


def _scan_rows(hre, him, lr, li, init, base, stride, steps):
    def step(j, st):
        sr, si = st
        r0 = pl.multiple_of(base + j * stride, SUBLANES)
        nr = lr * sr - li * si + hre[pl.ds(r0, SUBLANES), :]
        ni = lr * si + li * sr + him[pl.ds(r0, SUBLANES), :]
        hre[pl.ds(r0, SUBLANES), :] = nr
        him[pl.ds(r0, SUBLANES), :] = ni
        return nr, ni

    return lax.fori_loop(0, steps, step, init, unroll=4)


def _s5_project_in(u_ref, wb_ref, hre, him):
    for rows in (slice(0, BLK // 2), slice(BLK // 2, BLK)):
        bu = jnp.dot(u_ref[rows, :].astype(BF16), wb_ref[...], preferred_element_type=F32)
        hre[rows, :] = bu[:, :Q_ST]
        him[rows, :] = bu[:, Q_ST:]


def _s5_project_out(u_ref, wc_ref, d_ref, y_ref, hre, him):
    for rows in (slice(0, BLK // 2), slice(BLK // 2, BLK)):
        h = jnp.concatenate([hre[rows, :], him[rows, :]], axis=1).astype(BF16)
        y_ref[rows, :] = jnp.dot(h, wc_ref[...], preferred_element_type=F32) + d_ref[...] * u_ref[rows, :]


def _s5_prompt_kernel(u_ref, wb_ref, lam_ref, lams_ref, wc_ref, d_ref, y_ref, sfin_ref, hre, him, carry):
    @pl.when(pl.program_id(1) == 0)
    def _():
        carry[...] = jnp.zeros_like(carry)

    _s5_project_in(u_ref, wb_ref, hre, him)
    tile = (SUBLANES, Q_ST)
    lr = jnp.broadcast_to(lam_ref[0:1, :], tile)
    li = jnp.broadcast_to(lam_ref[1:2, :], tile)
    zero = jnp.zeros(tile, F32)
    steps = BLK // SUBLANES
    er, ei = _scan_rows(hre, him, lr, li, (zero, zero), 0, SUBLANES, steps)
    pr, pi = lams_ref[0:1, :], lams_ref[1:2, :]
    cr, ci = carry[0:1, :], carry[1:2, :]
    row = lax.broadcasted_iota(jnp.int32, tile, 0)
    sin_r, sin_i = zero, zero
    for s in range(SUBLANES):
        sin_r = jnp.where(row == s, cr, sin_r)
        sin_i = jnp.where(row == s, ci, sin_i)
        cr, ci = er[s:s + 1] + pr * cr - pi * ci, ei[s:s + 1] + pr * ci + pi * cr
    carry[0:1, :] = cr
    carry[1:2, :] = ci
    sfin_ref[0:1, :] = cr
    sfin_ref[1:2, :] = ci

    def fix(j, e):
        fr, fi = e
        nr = lr * fr - li * fi
        ni = lr * fi + li * fr
        r0 = pl.multiple_of(j * SUBLANES, SUBLANES)
        hre[pl.ds(r0, SUBLANES), :] += nr
        him[pl.ds(r0, SUBLANES), :] += ni
        return nr, ni

    lax.fori_loop(0, steps, fix, (sin_r, sin_i), unroll=4)
    _s5_project_out(u_ref, wc_ref, d_ref, y_ref, hre, him)


def _s5_sample_kernel(u_ref, wb_ref, lam_ref, wc_ref, d_ref, s0r_ref, s0i_ref, y_ref, sr_ref, si_ref, hre, him):
    _s5_project_in(u_ref, wb_ref, hre, him)
    tile = (SUBLANES, Q_ST)
    lr = jnp.broadcast_to(lam_ref[0:1, :], tile)
    li = jnp.broadcast_to(lam_ref[1:2, :], tile)
    for r in range(32 // SUBLANES):
        rows = slice(r * SUBLANES, (r + 1) * SUBLANES)
        er, ei = _scan_rows(hre, him, lr, li, (s0r_ref[rows, :], s0i_ref[rows, :]), r * SUBLANES, 32, 32)
        sr_ref[rows, :] = er
        si_ref[rows, :] = ei
    _s5_project_out(u_ref, wc_ref, d_ref, y_ref, hre, him)


def _s5_branch(h3, tabs, s0, *, prompt):
    m = h3.shape[1]
    u_spec = pl.BlockSpec((None, BLK, Q_IN), lambda q, i: (0, i, q))
    wb_spec = pl.BlockSpec((None, Q_IN, 2 * Q_ST), lambda q, i: (q, 0, 0))
    lam_spec = pl.BlockSpec((None, 2, Q_ST), lambda q, i: (q, 0, 0))
    wc_spec = pl.BlockSpec((None, 2 * Q_ST, Q_IN), lambda q, i: (q, 0, 0))
    d_spec = pl.BlockSpec((1, Q_IN), lambda q, i: (0, q))
    y_spec = pl.BlockSpec((BLK, Q_IN), lambda q, i: (i, q))
    scratch = [pltpu.VMEM((BLK, Q_ST), F32), pltpu.VMEM((BLK, Q_ST), F32)]
    if prompt:
        return pl.pallas_call(
            _s5_prompt_kernel,
            grid=(N_Q, m // BLK),
            in_specs=[u_spec, wb_spec, lam_spec, lam_spec, wc_spec, d_spec],
            out_specs=[y_spec, pl.BlockSpec((None, 2, Q_ST), lambda q, i: (q, 0, 0))],
            out_shape=[jax.ShapeDtypeStruct((m, W_BR), F32), jax.ShapeDtypeStruct((N_Q, 2, Q_ST), F32)],
            scratch_shapes=scratch + [pltpu.VMEM((SUBLANES, Q_ST), F32)],
            compiler_params=_params("parallel", "arbitrary"),
            name="s5_prompt",
        )(h3, tabs["wb"], tabs["lam"], tabs["lam_seg"], tabs["wc"], tabs["d"])
    st_spec = pl.BlockSpec((32, Q_ST), lambda q, i: (0, q))
    return pl.pallas_call(
        _s5_sample_kernel,
        grid=(N_Q, 1),
        in_specs=[u_spec, wb_spec, lam_spec, wc_spec, d_spec, st_spec, st_spec],
        out_specs=[y_spec, st_spec, st_spec],
        out_shape=[jax.ShapeDtypeStruct((m, W_BR), F32)] + [jax.ShapeDtypeStruct((32, N_Q * Q_ST), F32)] * 2,
        scratch_shapes=scratch,
        compiler_params=_params("parallel", "arbitrary"),
        name="s5_sample",
    )(h3, tabs["wb"], tabs["lam"], tabs["wc"], tabs["d"], s0[0], s0[1])


def _s5_tables(lam_re, lam_im, log_dt, b_re, b_im, c_re, c_im, d):
    dt = jnp.exp(log_dt)[:, None]
    mag = jnp.exp(lam_re * dt)
    lbr, lbi = mag * jnp.cos(lam_im * dt), mag * jnp.sin(lam_im * dt)
    den = lam_re * lam_re + lam_im * lam_im
    qr = ((lbr - 1.0) * lam_re + lbi * lam_im) / den
    qi = (lbi * lam_re - (lbr - 1.0) * lam_im) / den
    bbr = qr[:, :, None] * b_re - qi[:, :, None] * b_im
    bbi = qr[:, :, None] * b_im + qi[:, :, None] * b_re
    eye = jnp.eye(Q_GROUPS, dtype=F32)

    def block_in(b):
        return jnp.einsum("ab,qapi->qaibp", eye, b.reshape(N_Q, Q_GROUPS, 64, 16)).reshape(N_Q, Q_IN, Q_ST)

    def block_out(c):
        return jnp.einsum("ab,qaip->qapbi", eye, c.reshape(N_Q, Q_GROUPS, 16, 64)).reshape(N_Q, Q_ST, Q_IN)

    wb = jnp.concatenate([block_in(bbr), block_in(bbi)], axis=2).astype(BF16)
    wc = jnp.concatenate([block_out(c_re), -block_out(c_im)], axis=1).astype(BF16)
    sr, si = lbr, lbi
    for _ in range(7):
        sr, si = sr * sr - si * si, 2.0 * sr * si
    pack = lambda a, b: jnp.stack([a.reshape(N_Q, Q_ST), b.reshape(N_Q, Q_ST)], axis=1)
    return {"wb": wb, "wc": wc, "lam": pack(lbr, lbi), "lam_seg": pack(sr, si), "d": d.reshape(1, W_BR)}


def _load_resident(first_step, pairs, sems):
    @pl.when(first_step)
    def _():
        copies = [pltpu.make_async_copy(src, dst, sems.at[k]) for k, (src, dst) in enumerate(pairs)]
        for c in copies:
            c.start()
        for c in copies:
            c.wait()


def _sgu_kernel(k_hbm, v_ref, u_ref, bs_ref, yc_ref, k_ref, sems):
    _load_resident(pl.program_id(0) == 0, [(k_hbm, k_ref)], sems)
    for hd in range(SGU_HEADS):
        cols = slice(hd * HEAD_DIM, (hd + 1) * HEAD_DIM)
        mix = jnp.dot(k_ref[hd], v_ref[:, cols].astype(BF16), preferred_element_type=F32) + bs_ref[:, hd:hd + 1]
        yc_ref[:, cols] = (u_ref[:, cols] * mix).astype(BF16)


def _sgu_branch(h3, kmat, bias):
    m = h3.shape[1]
    plane = lambda k: pl.BlockSpec((None, BLK, W_BR), lambda i, k=k: (k, i, 0))
    return pl.pallas_call(
        _sgu_kernel,
        grid=(m // BLK,),
        in_specs=[
            pl.BlockSpec(memory_space=pl.ANY),
            plane(2),
            plane(1),
            pl.BlockSpec((BLK, SGU_HEADS), lambda i: (0, 0)),
        ],
        out_specs=pl.BlockSpec((BLK, W_BR), lambda i: (i, 0)),
        out_shape=jax.ShapeDtypeStruct((m, W_BR), BF16),
        scratch_shapes=[pltpu.VMEM((SGU_HEADS, BLK, BLK), BF16), pltpu.SemaphoreType.DMA((1,))],
        compiler_params=_params("arbitrary"),
        name="sgu",
    )(kmat, h3, h3, bias)


def _sgu_tables(w_s, b_s, steps, width):
    w = jnp.where(jnp.tril(jnp.ones((128, 128), dtype=bool)), w_s, 0)[:, :steps, :steps]
    row = jnp.arange(BLK)
    expand = (row[:, None] // width == jnp.arange(steps)[None, :]).astype(F32)
    same_slot = row[:, None] % width == row[None, :] % width
    spread = jnp.einsum("rj,hjk,ck->hrc", expand, w, expand)
    kmat = jnp.where(same_slot, spread, 0).astype(BF16)
    bias = jnp.repeat(b_s[:, :steps].T, width, axis=0)
    return kmat, bias


def _gates_kernel(xn_ref, w_ref, b_ref, o_ref):
    acc = jnp.dot(xn_ref[...], w_ref[...], preferred_element_type=F32) + b_ref[...]
    o_ref[...] = jax.nn.sigmoid(acc).astype(BF16)


def _gates(xn, w, b):
    m = xn.shape[0]
    tn = 1024
    per = D_MODEL // tn
    return pl.pallas_call(
        _gates_kernel,
        grid=(m // BLK, 3 * per),
        in_specs=[
            pl.BlockSpec((BLK, D_MODEL), lambda i, j: (i, 0)),
            pl.BlockSpec((D_MODEL, tn), lambda i, j: (0, j)),
            pl.BlockSpec((1, tn), lambda i, j: (0, j)),
        ],
        out_specs=pl.BlockSpec((None, BLK, tn), lambda i, j: (j // per, i, j % per)),
        out_shape=jax.ShapeDtypeStruct((3, m, D_MODEL), BF16),
        compiler_params=_params("parallel", "arbitrary"),
        name="gates",
    )(xn, w, b)


def _glu_kernel(ys_ref, w_ref, b_ref, o_ref):
    g = _gelu(ys_ref[...])
    gate = jax.nn.sigmoid(jnp.dot(g.astype(BF16), w_ref[...], preferred_element_type=F32) + b_ref[...])
    o_ref[...] = (g * gate).astype(BF16)


def _glu(ys, w, b):
    m = ys.shape[0]
    return pl.pallas_call(
        _glu_kernel,
        grid=(m // BLK,),
        in_specs=[
            pl.BlockSpec((BLK, W_BR), lambda i: (i, 0)),
            pl.BlockSpec((W_BR, W_BR), lambda i: (0, 0)),
            pl.BlockSpec((1, W_BR), lambda i: (0, 0)),
        ],
        out_specs=pl.BlockSpec((BLK, W_BR), lambda i: (i, 0)),
        out_shape=jax.ShapeDtypeStruct((m, W_BR), BF16),
        compiler_params=_params("parallel"),
        name="glu",
    )(ys, w, b)


MERGE_TM = 512
MERGE_TN = 1024
MERGE_NT = D_MODEL // MERGE_TN


def _merge_project_kernel(ya_ref, yb_ref, yc_ref, wa_hbm, wb_hbm, wc_hbm, g0_ref, g1_ref, g2_ref,
                          wo_hbm, x_ref, o_ref, merged_ref, wa_ref, wb_ref, wc_ref, wo_ref, sems):
    j = pl.program_id(1)
    _load_resident((pl.program_id(0) == 0) & (j == 0),
                   [(wa_hbm, wa_ref), (wb_hbm, wb_ref), (wc_hbm, wc_ref), (wo_hbm, wo_ref)], sems)

    for t in range(MERGE_NT):
        cols = slice(t * MERGE_TN, (t + 1) * MERGE_TN)

        @pl.when(j == t)
        def _():
            dot = lambda a, w: jnp.dot(a[...], w[:, cols], preferred_element_type=F32)
            merged = (g0_ref[...] * dot(ya_ref, wa_ref) + g1_ref[...] * dot(yb_ref, wb_ref)
                      + g2_ref[...] * dot(yc_ref, wc_ref))
            merged_ref[:, cols] = merged.astype(BF16)

        @pl.when(j == MERGE_NT + t)
        def _():
            o_ref[...] = x_ref[...] + jnp.dot(merged_ref[...], wo_ref[:, cols], preferred_element_type=F32)


def _merge_project(ya, yb, yc, wa, wb, wc, g3, wo, x):
    m = ya.shape[0]
    tm, tn, nt = MERGE_TM, MERGE_TN, MERGE_NT
    first = lambda j: jnp.minimum(j, nt - 1)
    second = lambda j: jnp.maximum(j - nt, 0)
    y_spec = pl.BlockSpec((tm, W_BR), lambda i, j: (i, 0))
    hbm = pl.BlockSpec(memory_space=pl.ANY)
    gate = lambda k: pl.BlockSpec((None, tm, tn), lambda i, j, k=k: (k, i, first(j)))
    branch_w = pltpu.VMEM((W_BR, D_MODEL), BF16)
    return pl.pallas_call(
        _merge_project_kernel,
        grid=(m // tm, 2 * nt),
        in_specs=[y_spec, y_spec, y_spec, hbm, hbm, hbm, gate(0), gate(1), gate(2), hbm,
                  pl.BlockSpec((tm, tn), lambda i, j: (i, second(j)))],
        out_specs=pl.BlockSpec((tm, tn), lambda i, j: (i, second(j))),
        out_shape=jax.ShapeDtypeStruct((m, D_MODEL), F32),
        scratch_shapes=[pltpu.VMEM((tm, D_MODEL), BF16), branch_w, branch_w, branch_w,
                        pltpu.VMEM((D_MODEL, D_MODEL), BF16), pltpu.SemaphoreType.DMA((4,))],
        compiler_params=_params("arbitrary", "arbitrary"),
        name="merge_project",
    )(ya, yb, yc, wa, wb, wc, g3, g3, g3, wo, x)


def _ffn_kernel(x_ref, g_ref, wg_ref, wu_ref, wo_ref, o_ref, xn_ref):
    @pl.when(pl.program_id(1) == 0)
    def _():
        x = x_ref[...]
        xn_ref[...] = _rms(x, g_ref[...]).astype(BF16)
        o_ref[...] = x

    xn = xn_ref[...]
    gate = jnp.dot(xn, wg_ref[...], preferred_element_type=F32)
    up = jnp.dot(xn, wu_ref[...], preferred_element_type=F32)
    act = (jax.nn.silu(gate) * up).astype(BF16)
    o_ref[...] += jnp.dot(act, wo_ref[...], preferred_element_type=F32)


def _ffn(x, g, w_in, w_out):
    m = x.shape[0]
    tm, tf = 512, 512
    nf = D_FF // tf
    return pl.pallas_call(
        _ffn_kernel,
        grid=(m // tm, nf),
        in_specs=[
            pl.BlockSpec((tm, D_MODEL), lambda i, j: (i, 0)),
            pl.BlockSpec((1, D_MODEL), lambda i, j: (0, 0)),
            pl.BlockSpec((D_MODEL, tf), lambda i, j: (0, j)),
            pl.BlockSpec((D_MODEL, tf), lambda i, j: (0, j + nf)),
            pl.BlockSpec((tf, D_MODEL), lambda i, j: (j, 0)),
        ],
        out_specs=pl.BlockSpec((tm, D_MODEL), lambda i, j: (i, 0)),
        out_shape=jax.ShapeDtypeStruct((m, D_MODEL), F32),
        scratch_shapes=[pltpu.VMEM((tm, D_MODEL), BF16)],
        compiler_params=_params("parallel", "arbitrary"),
        name="ffn",
    )(x, g, w_in, w_in, w_out)


def _final_norm_kernel(x_ref, g_ref, o_ref):
    o_ref[...] = _rms(x_ref[...], g_ref[...])


def _final_norm(x, g):
    m = x.shape[0]
    return pl.pallas_call(
        _final_norm_kernel,
        grid=(m // BLK,),
        in_specs=[pl.BlockSpec((BLK, D_MODEL), lambda i: (i, 0)), pl.BlockSpec((1, D_MODEL), lambda i: (0, 0))],
        out_specs=pl.BlockSpec((BLK, D_MODEL), lambda i: (i, 0)),
        out_shape=jax.ShapeDtypeStruct((m, D_MODEL), F32),
        compiler_params=_params("parallel"),
        name="final_norm",
    )(x, g)


def _layer(x, p, sgu_tabs, conv_bufs, s0, *, prompt):
    h3, xn = _in_proj(x, p["g_mix"], p["w_in_bc"], p["ln_v_g"])
    g3 = _gates(xn, p["w_gate"], p["b_gate"])
    ya, conv_new = _conv_branch(xn, p["w_in_a"], p["conv_w"], conv_bufs, prompt=prompt)
    ys, *state = _s5_branch(h3, p["s5"], s0, prompt=prompt)
    yb = _glu(ys, p["w_glu"], p["b_glu"])
    yc = _sgu_branch(h3, *sgu_tabs)
    x = _merge_project(ya, yb, yc, p["w_conv_out"], p["w_ssm_out"], p["w_sgu_out"], g3, p["w_o"], x)
    x = _ffn(x, p["g_ffn"], p["w_ffn_in"], p["w_ffn_out"])
    return x, conv_new, state, h3[2]


def kernel(x_prompt, x_sample, cache_conv, state_ssm_re, state_ssm_im, norm_mix_g, w_in, conv_w, w_conv_out, ssm_lam_re, ssm_lam_im, ssm_log_dt, ssm_b_re, ssm_b_im, ssm_c_re, ssm_c_im, ssm_d, w_glu, b_glu, w_ssm_out, ln_v_g, w_sgu_s, b_sgu_s, w_sgu_out, w_gate, b_gate, w_o, norm_ffn_g, w_ffn_in, w_ffn_out, norm_final_g):
    seq = x_prompt.shape[1]
    nb = seq // BLK
    xp = x_prompt.reshape(nb, SUBLANES, BLK // SUBLANES, D_MODEL).transpose(0, 2, 1, 3).reshape(seq, D_MODEL)
    xs = x_sample.transpose(1, 0, 2).reshape(BLK, D_MODEL)

    conv_p, re_p, im_p, conv_s, re_s, im_s, v_s = [], [], [], [], [], [], []
    for l in range(DEPTH):
        p = {
            "g_mix": norm_mix_g[l].reshape(1, D_MODEL),
            "w_in_a": _conv_weight(w_in[l]),
            "w_in_bc": w_in[l][:, 3 * W_BR:].astype(BF16),
            "w_gate": w_gate[l].reshape(D_MODEL, 3 * D_MODEL).astype(BF16),
            "b_gate": b_gate[l].reshape(1, 3 * D_MODEL),
            "conv_w": conv_w[l],
            "s5": _s5_tables(ssm_lam_re[l], ssm_lam_im[l], ssm_log_dt[l], ssm_b_re[l], ssm_b_im[l],
                             ssm_c_re[l], ssm_c_im[l], ssm_d[l]),
            "w_glu": w_glu[l].astype(BF16),
            "b_glu": b_glu[l].reshape(1, W_BR),
            "ln_v_g": ln_v_g[l].reshape(1, W_BR),
            "w_conv_out": w_conv_out[l].astype(BF16),
            "w_ssm_out": w_ssm_out[l].astype(BF16),
            "w_sgu_out": w_sgu_out[l].astype(BF16),
            "w_o": w_o[l].astype(BF16),
            "g_ffn": norm_ffn_g[l].reshape(1, D_MODEL),
            "w_ffn_in": w_ffn_in[l].astype(BF16),
            "w_ffn_out": w_ffn_out[l].astype(BF16),
        }
        sgu_p = _sgu_tables(w_sgu_s[l], b_sgu_s[l], BLK // SUBLANES, SUBLANES)
        sgu_s = _sgu_tables(w_sgu_s[l], b_sgu_s[l], 32, 32)
        xp, cp, sp, _ = _layer(xp, p, sgu_p, None, None, prompt=True)
        bufs = (cache_conv[l, :, 0, :], cache_conv[l, :, 1, :])
        s0 = (state_ssm_re[l].reshape(32, N_Q * Q_ST), state_ssm_im[l].reshape(32, N_Q * Q_ST))
        xs, cs, ss, vs = _layer(xs, p, sgu_s, bufs, s0, prompt=False)
        conv_p.append(cp[-1, 0:2].reshape(1, 2, W_BR))
        re_p.append(sp[0][:, 0, :].reshape(1, 64, 64))
        im_p.append(sp[0][:, 1, :].reshape(1, 64, 64))
        conv_s.append(cs.reshape(2, 32, W_BR).transpose(1, 0, 2))
        re_s.append(ss[0].reshape(32, 64, 64))
        im_s.append(ss[1].reshape(32, 64, 64))
        v_s.append(vs.reshape(32, 32, W_BR).transpose(1, 0, 2))

    g_fin = norm_final_g.reshape(1, D_MODEL)
    yp = _final_norm(xp, g_fin)
    ys = _final_norm(xs, g_fin)
    y_prompt = yp.reshape(nb, BLK // SUBLANES, SUBLANES, D_MODEL).transpose(0, 2, 1, 3).reshape(1, seq, D_MODEL)
    y_sample = ys.reshape(32, 32, D_MODEL).transpose(1, 0, 2)
    return (y_prompt, y_sample, jnp.stack(conv_p), jnp.stack(re_p), jnp.stack(im_p),
            jnp.stack(conv_s), jnp.stack(re_s), jnp.stack(im_s), jnp.stack(v_s))
```

```python
import functools

import jax
import jax.numpy as jnp
from jax import lax
from jax.experimental import pallas as pl
from jax.experimental.pallas import tpu as pltpu

F32 = jnp.float32
BF16 = jnp.bfloat16

D_MODEL = 2048
DEPTH = 4
W_BR = 1024
SSM_GROUPS = 64
N_Q = 4
Q_GROUPS = SSM_GROUPS // N_Q
Q_IN = Q_GROUPS * 16
Q_ST = Q_GROUPS * 64
SGU_HEADS = 8
HEAD_DIM = 128
D_FF = 5632
EPS = 1e-6
BLK = 1024
SUBLANES = 8
CONV_TC = 256
VMEM_LIMIT_BYTES = 56 * 1024 * 1024


def _params(*sem):
    return pltpu.CompilerParams(dimension_semantics=sem, vmem_limit_bytes=VMEM_LIMIT_BYTES)


def _gelu(x):
    return jax.nn.gelu(x, approximate=True)


def _rms(x, g):
    return x * lax.rsqrt(jnp.mean(x * x, axis=-1, keepdims=True) + EPS) * g


def _load_resident(first_step, pairs, sems):
    @pl.when(first_step)
    def _():
        copies = [pltpu.make_async_copy(src, dst, sems.at[k]) for k, (src, dst) in enumerate(pairs)]
        for c in copies:
            c.start()
        for c in copies:
            c.wait()


def _in_proj_kernel(x_ref, g_ref, w_hbm, lng_ref, h_ref, xn_ref, w_ref, sems, *, layer):
    w_src = w_hbm.at[layer, :, pl.ds(3 * W_BR, 3 * W_BR)]
    _load_resident(pl.program_id(0) == 0, [(w_src, w_ref)], sems)
    xn = _rms(x_ref[...], g_ref[...]).astype(BF16)
    xn_ref[...] = xn
    plane = lambda k: jnp.dot(xn, w_ref[:, k * W_BR:(k + 1) * W_BR], preferred_element_type=F32)
    h_ref[0] = plane(0)
    h_ref[1] = _gelu(plane(1))
    gv = _gelu(plane(2))
    xc = gv - jnp.mean(gv, axis=-1, keepdims=True)
    h_ref[2] = xc * lax.rsqrt(jnp.mean(xc * xc, axis=-1, keepdims=True) + EPS) * lng_ref[...]


def _in_proj(x, g, w_all, layer, ln_g):
    m = x.shape[0]
    tm = 512
    return pl.pallas_call(
        functools.partial(_in_proj_kernel, layer=layer),
        grid=(m // tm,),
        in_specs=[
            pl.BlockSpec((tm, D_MODEL), lambda i: (i, 0)),
            pl.BlockSpec((1, D_MODEL), lambda i: (0, 0)),
            pl.BlockSpec(memory_space=pl.ANY),
            pl.BlockSpec((1, W_BR), lambda i: (0, 0)),
        ],
        out_specs=[
            pl.BlockSpec((3, tm, W_BR), lambda i: (0, i, 0)),
            pl.BlockSpec((tm, D_MODEL), lambda i: (i, 0)),
        ],
        out_shape=[jax.ShapeDtypeStruct((3, m, W_BR), F32), jax.ShapeDtypeStruct((m, D_MODEL), BF16)],
        scratch_shapes=[pltpu.VMEM((D_MODEL, 3 * W_BR), BF16), pltpu.SemaphoreType.DMA((1,))],
        compiler_params=_params("arbitrary"),
        name="in_proj",
    )(x, g, w_all, ln_g)


def _conv_taps(z, b1, b2, w, r):
    prev1 = jnp.concatenate([b1, z[:BLK - r]], axis=0)
    prev2 = jnp.concatenate([b2, b1, z[:BLK - 2 * r]], axis=0)
    return w[0:1] * prev2 + w[1:2] * prev1 + w[2:3] * z


def _conv_split(xn_ref, wb_ref, wc_ref, wh_ref):
    xn = xn_ref[...]
    dot = lambda w: jnp.dot(xn, w[...], preferred_element_type=F32)
    return dot(wb_ref), dot(wc_ref) * dot(wh_ref)


def _conv_prompt_kernel(xn_ref, wb_ref, wc_ref, wh_ref, cw_ref, ya_ref, cn_ref, carry_ref):
    r = SUBLANES
    j = pl.program_id(1)

    @pl.when(pl.program_id(0) == 0)
    def _():
        carry_ref[j] = jnp.zeros(carry_ref.shape[1:], F32)

    b, z = _conv_split(xn_ref, wb_ref, wc_ref, wh_ref)
    last1 = z[BLK - r:]
    last2 = z[BLK - 2 * r:BLK - r]
    row = lax.broadcasted_iota(jnp.int32, last1.shape, 0)
    carry = carry_ref[j]
    b1 = jnp.where(row == 0, carry[1:2, :], pltpu.roll(last1, 1, 0))
    b2 = jnp.where(row == 0, carry[0:1, :], pltpu.roll(last2, 1, 0))
    ya_ref[...] = (b * _conv_taps(z, b1, b2, cw_ref[...], r)).astype(BF16)
    tail = jnp.where(row == 0, last2[r - 1:r], last1[r - 1:r])
    carry_ref[j] = tail
    cn_ref[...] = tail


def _conv_sample_kernel(xn_ref, wb_ref, wc_ref, wh_ref, cw_ref, buf0_ref, buf1_ref, ya_ref, cn_ref):
    r = 32
    b, z = _conv_split(xn_ref, wb_ref, wc_ref, wh_ref)
    ya_ref[...] = (b * _conv_taps(z, buf1_ref[...], buf0_ref[...], cw_ref[...], r)).astype(BF16)
    cn_ref[...] = z[BLK - 2 * r:]


def _conv_branch(xn, w, layer, conv_w, bufs, *, prompt):
    m = xn.shape[0]
    tc = CONV_TC
    nt = W_BR // tc
    w_spec = lambda k: pl.BlockSpec((None, D_MODEL, tc), lambda i, j, k=k: (layer, 0, k * nt + j))
    in_specs = [
        pl.BlockSpec((BLK, D_MODEL), lambda i, j: (i, 0)),
        w_spec(0), w_spec(1), w_spec(2),
        pl.BlockSpec((3, tc), lambda i, j: (0, j)),
    ]
    ya_spec = pl.BlockSpec((BLK, tc), lambda i, j: (i, j))
    if prompt:
        return pl.pallas_call(
            _conv_prompt_kernel,
            grid=(m // BLK, nt),
            in_specs=in_specs,
            out_specs=[ya_spec, pl.BlockSpec((None, SUBLANES, tc), lambda i, j: (i, 0, j))],
            out_shape=[jax.ShapeDtypeStruct((m, W_BR), BF16),
                       jax.ShapeDtypeStruct((m // BLK, SUBLANES, W_BR), F32)],
            scratch_shapes=[pltpu.VMEM((nt, SUBLANES, tc), F32)],
            compiler_params=_params("arbitrary", "arbitrary"),
            name="conv_prompt",
        )(xn, w, w, w, conv_w)
    buf_spec = pl.BlockSpec((32, tc), lambda i, j: (0, j))
    return pl.pallas_call(
        _conv_sample_kernel,
        grid=(1, nt),
        in_specs=in_specs + [buf_spec, buf_spec],
        out_specs=[ya_spec, pl.BlockSpec((64, tc), lambda i, j: (0, j))],
        out_shape=[jax.ShapeDtypeStruct((m, W_BR), BF16), jax.ShapeDtypeStruct((64, W_BR), F32)],
        compiler_params=_params("arbitrary", "arbitrary"),
        name="conv_sample",
    )(xn, w, w, w, conv_w, bufs[0], bufs[1])


def _scan_rows(hre, him, lr, li, init, base, stride, steps):
    def step(j, st):
        sr, si = st
        r0 = pl.multiple_of(base + j * stride, SUBLANES)
        nr = lr * sr - li * si + hre[pl.ds(r0, SUBLANES), :]
        ni = lr * si + li * sr + him[pl.ds(r0, SUBLANES), :]
        hre[pl.ds(r0, SUBLANES), :] = nr
        him[pl.ds(r0, SUBLANES), :] = ni
        return nr, ni

    return lax.fori_loop(0, steps, step, init, unroll=4)


def _s5_project_in(u_ref, wb_ref, hre, him):
    for rows in (slice(0, BLK // 2), slice(BLK // 2, BLK)):
        bu = jnp.dot(u_ref[rows, :].astype(BF16), wb_ref[...], preferred_element_type=F32)
        hre[rows, :] = bu[:, :Q_ST]
        him[rows, :] = bu[:, Q_ST:]


def _s5_project_out(u_ref, wc_ref, d_ref, y_ref, hre, him):
    for rows in (slice(0, BLK // 2), slice(BLK // 2, BLK)):
        h = jnp.concatenate([hre[rows, :], him[rows, :]], axis=1).astype(BF16)
        y_ref[rows, :] = jnp.dot(h, wc_ref[...], preferred_element_type=F32) + d_ref[...] * u_ref[rows, :]


def _s5_prompt_kernel(u_ref, wb_ref, lam_ref, lams_ref, wc_ref, d_ref, y_ref, sfin_ref, hre, him, carry):
    @pl.when(pl.program_id(1) == 0)
    def _():
        carry[...] = jnp.zeros_like(carry)

    _s5_project_in(u_ref, wb_ref, hre, him)
    tile = (SUBLANES, Q_ST)
    lr = jnp.broadcast_to(lam_ref[0:1, :], tile)
    li = jnp.broadcast_to(lam_ref[1:2, :], tile)
    zero = jnp.zeros(tile, F32)
    steps = BLK // SUBLANES
    er, ei = _scan_rows(hre, him, lr, li, (zero, zero), 0, SUBLANES, steps)
    pr, pi = lams_ref[0:1, :], lams_ref[1:2, :]
    cr, ci = carry[0:1, :], carry[1:2, :]
    row = lax.broadcasted_iota(jnp.int32, tile, 0)
    sin_r, sin_i = zero, zero
    for s in range(SUBLANES):
        sin_r = jnp.where(row == s, cr, sin_r)
        sin_i = jnp.where(row == s, ci, sin_i)
        cr, ci = er[s:s + 1] + pr * cr - pi * ci, ei[s:s + 1] + pr * ci + pi * cr
    carry[0:1, :] = cr
    carry[1:2, :] = ci
    sfin_ref[0:1, :] = cr
    sfin_ref[1:2, :] = ci

    def fix(j, e):
        fr, fi = e
        nr = lr * fr - li * fi
        ni = lr * fi + li * fr
        r0 = pl.multiple_of(j * SUBLANES, SUBLANES)
        hre[pl.ds(r0, SUBLANES), :] += nr
        him[pl.ds(r0, SUBLANES), :] += ni
        return nr, ni

    lax.fori_loop(0, steps, fix, (sin_r, sin_i), unroll=4)
    _s5_project_out(u_ref, wc_ref, d_ref, y_ref, hre, him)


def _s5_sample_kernel(u_ref, wb_ref, lam_ref, wc_ref, d_ref, s0r_ref, s0i_ref, y_ref, sr_ref, si_ref, hre, him):
    _s5_project_in(u_ref, wb_ref, hre, him)
    tile = (SUBLANES, Q_ST)
    lr = jnp.broadcast_to(lam_ref[0:1, :], tile)
    li = jnp.broadcast_to(lam_ref[1:2, :], tile)
    for r in range(32 // SUBLANES):
        rows = slice(r * SUBLANES, (r + 1) * SUBLANES)
        er, ei = _scan_rows(hre, him, lr, li, (s0r_ref[rows, :], s0i_ref[rows, :]), r * SUBLANES, 32, 32)
        sr_ref[rows, :] = er
        si_ref[rows, :] = ei
    _s5_project_out(u_ref, wc_ref, d_ref, y_ref, hre, him)


def _s5_branch(h3, tabs, s0, *, prompt):
    m = h3.shape[1]
    u_spec = pl.BlockSpec((None, BLK, Q_IN), lambda q, i: (0, i, q))
    wb_spec = pl.BlockSpec((None, Q_IN, 2 * Q_ST), lambda q, i: (q, 0, 0))
    lam_spec = pl.BlockSpec((None, 2, Q_ST), lambda q, i: (q, 0, 0))
    wc_spec = pl.BlockSpec((None, 2 * Q_ST, Q_IN), lambda q, i: (q, 0, 0))
    d_spec = pl.BlockSpec((1, Q_IN), lambda q, i: (0, q))
    y_spec = pl.BlockSpec((BLK, Q_IN), lambda q, i: (i, q))
    scratch = [pltpu.VMEM((BLK, Q_ST), F32), pltpu.VMEM((BLK, Q_ST), F32)]
    if prompt:
        return pl.pallas_call(
            _s5_prompt_kernel,
            grid=(N_Q, m // BLK),
            in_specs=[u_spec, wb_spec, lam_spec, lam_spec, wc_spec, d_spec],
            out_specs=[y_spec, pl.BlockSpec((None, 2, Q_ST), lambda q, i: (q, 0, 0))],
            out_shape=[jax.ShapeDtypeStruct((m, W_BR), F32), jax.ShapeDtypeStruct((N_Q, 2, Q_ST), F32)],
            scratch_shapes=scratch + [pltpu.VMEM((SUBLANES, Q_ST), F32)],
            compiler_params=_params("parallel", "arbitrary"),
            name="s5_prompt",
        )(h3, tabs["wb"], tabs["lam"], tabs["lam_seg"], tabs["wc"], tabs["d"])
    st_spec = pl.BlockSpec((32, Q_ST), lambda q, i: (0, q))
    return pl.pallas_call(
        _s5_sample_kernel,
        grid=(N_Q, 1),
        in_specs=[u_spec, wb_spec, lam_spec, wc_spec, d_spec, st_spec, st_spec],
        out_specs=[y_spec, st_spec, st_spec],
        out_shape=[jax.ShapeDtypeStruct((m, W_BR), F32)] + [jax.ShapeDtypeStruct((32, N_Q * Q_ST), F32)] * 2,
        scratch_shapes=scratch,
        compiler_params=_params("parallel", "arbitrary"),
        name="s5_sample",
    )(h3, tabs["wb"], tabs["lam"], tabs["wc"], tabs["d"], s0[0], s0[1])


def _s5_tables(lam_re, lam_im, log_dt, b_re, b_im, c_re, c_im, d):
    dt = jnp.exp(log_dt)[:, None]
    mag = jnp.exp(lam_re * dt)
    lbr, lbi = mag * jnp.cos(lam_im * dt), mag * jnp.sin(lam_im * dt)
    den = lam_re * lam_re + lam_im * lam_im
    qr = ((lbr - 1.0) * lam_re + lbi * lam_im) / den
    qi = (lbi * lam_re - (lbr - 1.0) * lam_im) / den
    bbr = qr[:, :, None] * b_re - qi[:, :, None] * b_im
    bbi = qr[:, :, None] * b_im + qi[:, :, None] * b_re
    eye = jnp.eye(Q_GROUPS, dtype=F32)

    def block_in(b):
        return jnp.einsum("ab,qapi->qaibp", eye, b.reshape(N_Q, Q_GROUPS, 64, 16)).reshape(N_Q, Q_IN, Q_ST)

    def block_out(c):
        return jnp.einsum("ab,qaip->qapbi", eye, c.reshape(N_Q, Q_GROUPS, 16, 64)).reshape(N_Q, Q_ST, Q_IN)

    wb = jnp.concatenate([block_in(bbr), block_in(bbi)], axis=2).astype(BF16)
    wc = jnp.concatenate([block_out(c_re), -block_out(c_im)], axis=1).astype(BF16)
    sr, si = lbr, lbi
    for _ in range(7):
        sr, si = sr * sr - si * si, 2.0 * sr * si
    pack = lambda a, b: jnp.stack([a.reshape(N_Q, Q_ST), b.reshape(N_Q, Q_ST)], axis=1)
    return {"wb": wb, "wc": wc, "lam": pack(lbr, lbi), "lam_seg": pack(sr, si), "d": d.reshape(1, W_BR)}


def _sgu_kernel(k_hbm, v_ref, u_ref, bs_ref, yc_ref, k_ref, sems, *, layer):
    _load_resident(pl.program_id(0) == 0, [(k_hbm.at[layer], k_ref)], sems)
    for hd in range(SGU_HEADS):
        cols = slice(hd * HEAD_DIM, (hd + 1) * HEAD_DIM)
        mix = jnp.dot(k_ref[hd], v_ref[:, cols].astype(BF16), preferred_element_type=F32) + bs_ref[:, hd:hd + 1]
        yc_ref[:, cols] = (u_ref[:, cols] * mix).astype(BF16)


def _sgu_branch(h3, kmat_all, bias_all, layer):
    m = h3.shape[1]
    plane = lambda k: pl.BlockSpec((None, BLK, W_BR), lambda i, k=k: (k, i, 0))
    return pl.pallas_call(
        functools.partial(_sgu_kernel, layer=layer),
        grid=(m // BLK,),
        in_specs=[
            pl.BlockSpec(memory_space=pl.ANY),
            plane(2),
            plane(1),
            pl.BlockSpec((None, BLK, SGU_HEADS), lambda i: (layer, 0, 0)),
        ],
        out_specs=pl.BlockSpec((BLK, W_BR), lambda i: (i, 0)),
        out_shape=jax.ShapeDtypeStruct((m, W_BR), BF16),
        scratch_shapes=[pltpu.VMEM((SGU_HEADS, BLK, BLK), BF16), pltpu.SemaphoreType.DMA((1,))],
        compiler_params=_params("arbitrary"),
        name="sgu",
    )(kmat_all, h3, h3, bias_all)


def _sgu_tables(w_s, b_s, steps, width):
    w = jnp.where(jnp.tril(jnp.ones((128, 128), dtype=bool)), w_s, 0)[:, :, :steps, :steps]
    row = jnp.arange(BLK)
    expand = (row[:, None] // width == jnp.arange(steps)[None, :]).astype(F32)
    same_slot = row[:, None] % width == row[None, :] % width
    spread = jnp.einsum("rj,lhjk,ck->lhrc", expand, w, expand)
    kmat = jnp.where(same_slot, spread, 0).astype(BF16)
    bias = jnp.repeat(b_s[:, :, :steps].transpose(0, 2, 1), width, axis=1)
    return kmat, bias


def _gates_kernel(xn_ref, w_ref, b_ref, o_ref):
    acc = jnp.dot(xn_ref[...], w_ref[...], preferred_element_type=F32) + b_ref[...]
    o_ref[...] = jax.nn.sigmoid(acc).astype(BF16)


def _gates(xn, w_all, layer, b):
    m = xn.shape[0]
    tn = MERGE_TN
    per = D_MODEL // tn
    return pl.pallas_call(
        _gates_kernel,
        grid=(3 * per, m // BLK),
        in_specs=[
            pl.BlockSpec((BLK, D_MODEL), lambda j, i: (i, 0)),
            pl.BlockSpec((None, None, D_MODEL, tn), lambda j, i: (layer, j // per, 0, j % per)),
            pl.BlockSpec((1, tn), lambda j, i: (0, j)),
        ],
        out_specs=pl.BlockSpec((None, None, BLK, tn), lambda j, i: (j // per, j % per, i, 0)),
        out_shape=jax.ShapeDtypeStruct((3, per, m, tn), BF16),
        compiler_params=_params("parallel", "parallel"),
        name="gates",
    )(xn, w_all, b)


def _glu_kernel(ys_ref, w_ref, b_ref, o_ref):
    g = _gelu(ys_ref[...])
    gate = jax.nn.sigmoid(jnp.dot(g.astype(BF16), w_ref[...], preferred_element_type=F32) + b_ref[...])
    o_ref[...] = (g * gate).astype(BF16)


def _glu(ys, w_all, layer, b):
    m = ys.shape[0]
    return pl.pallas_call(
        _glu_kernel,
        grid=(m // BLK,),
        in_specs=[
            pl.BlockSpec((BLK, W_BR), lambda i: (i, 0)),
            pl.BlockSpec((None, W_BR, W_BR), lambda i: (layer, 0, 0)),
            pl.BlockSpec((1, W_BR), lambda i: (0, 0)),
        ],
        out_specs=pl.BlockSpec((BLK, W_BR), lambda i: (i, 0)),
        out_shape=jax.ShapeDtypeStruct((m, W_BR), BF16),
        compiler_params=_params("parallel"),
        name="glu",
    )(ys, w_all, b)


MERGE_TM = 512
MERGE_TN = 1024
MERGE_NT = D_MODEL // MERGE_TN


def _merge_project_kernel(ya_ref, yb_ref, yc_ref, wa_hbm, wb_hbm, wc_hbm, g0_ref, g1_ref, g2_ref,
                          wo_hbm, x_ref, o_ref, merged_ref, wa_ref, wb_ref, wc_ref, wo_ref, sems, *, layer):
    j = pl.program_id(1)
    _load_resident((pl.program_id(0) == 0) & (j == 0),
                   [(wa_hbm.at[layer], wa_ref), (wb_hbm.at[layer], wb_ref), (wc_hbm.at[layer], wc_ref),
                    (wo_hbm.at[layer], wo_ref)], sems)

    for t in range(MERGE_NT):
        cols = slice(t * MERGE_TN, (t + 1) * MERGE_TN)

        @pl.when(j == t)
        def _():
            dot = lambda a, w: jnp.dot(a[...], w[:, cols], preferred_element_type=F32)
            merged = (g0_ref[...] * dot(ya_ref, wa_ref) + g1_ref[...] * dot(yb_ref, wb_ref)
                      + g2_ref[...] * dot(yc_ref, wc_ref))
            merged_ref[:, cols] = merged.astype(BF16)

        @pl.when(j == MERGE_NT + t)
        def _():
            o_ref[...] = x_ref[...] + jnp.dot(merged_ref[...], wo_ref[:, cols], preferred_element_type=F32)


def _merge_project(ya, yb, yc, wa, wb, wc, g3, wo, x, layer):
    m = ya.shape[0]
    tm, tn, nt = MERGE_TM, MERGE_TN, MERGE_NT
    first = lambda j: jnp.minimum(j, nt - 1)
    second = lambda j: jnp.maximum(j - nt, 0)
    y_spec = pl.BlockSpec((tm, W_BR), lambda i, j: (i, 0))
    hbm = pl.BlockSpec(memory_space=pl.ANY)
    gate = lambda k: pl.BlockSpec((None, None, tm, tn), lambda i, j, k=k: (k, first(j), i, 0))
    branch_w = pltpu.VMEM((W_BR, D_MODEL), BF16)
    return pl.pallas_call(
        functools.partial(_merge_project_kernel, layer=layer),
        grid=(m // tm, 2 * nt),
        in_specs=[y_spec, y_spec, y_spec, hbm, hbm, hbm, gate(0), gate(1), gate(2), hbm,
                  pl.BlockSpec((tm, tn), lambda i, j: (i, second(j)))],
        out_specs=pl.BlockSpec((tm, tn), lambda i, j: (i, second(j))),
        out_shape=jax.ShapeDtypeStruct((m, D_MODEL), F32),
        scratch_shapes=[pltpu.VMEM((tm, D_MODEL), BF16), branch_w, branch_w, branch_w,
                        pltpu.VMEM((D_MODEL, D_MODEL), BF16), pltpu.SemaphoreType.DMA((4,))],
        compiler_params=_params("arbitrary", "arbitrary"),
        name="merge_project",
    )(ya, yb, yc, wa, wb, wc, g3, g3, g3, wo, x)


def _ffn_kernel(x_ref, g_ref, gout_ref, wg_ref, wu_ref, wo_ref, o_ref, xn_ref, *, norm_out):
    j = pl.program_id(1)

    @pl.when(j == 0)
    def _():
        x = x_ref[...]
        xn_ref[...] = _rms(x, g_ref[...]).astype(BF16)
        o_ref[...] = x

    xn = xn_ref[...]
    gate = jnp.dot(xn, wg_ref[...], preferred_element_type=F32)
    up = jnp.dot(xn, wu_ref[...], preferred_element_type=F32)
    act = (jax.nn.silu(gate) * up).astype(BF16)
    o_ref[...] += jnp.dot(act, wo_ref[...], preferred_element_type=F32)

    if norm_out:
        @pl.when(j == pl.num_programs(1) - 1)
        def _():
            o_ref[...] = _rms(o_ref[...], gout_ref[...])


def _ffn(x, g, w_in, w_out, layer, g_out, *, norm_out):
    m = x.shape[0]
    tm, tf = 512, 512
    nf = D_FF // tf
    return pl.pallas_call(
        functools.partial(_ffn_kernel, norm_out=norm_out),
        grid=(m // tm, nf),
        in_specs=[
            pl.BlockSpec((tm, D_MODEL), lambda i, j: (i, 0)),
            pl.BlockSpec((1, D_MODEL), lambda i, j: (0, 0)),
            pl.BlockSpec((1, D_MODEL), lambda i, j: (0, 0)),
            pl.BlockSpec((None, D_MODEL, tf), lambda i, j: (layer, 0, j)),
            pl.BlockSpec((None, D_MODEL, tf), lambda i, j: (layer, 0, j + nf)),
            pl.BlockSpec((None, tf, D_MODEL), lambda i, j: (layer, j, 0)),
        ],
        out_specs=pl.BlockSpec((tm, D_MODEL), lambda i, j: (i, 0)),
        out_shape=jax.ShapeDtypeStruct((m, D_MODEL), F32),
        scratch_shapes=[pltpu.VMEM((tm, D_MODEL), BF16)],
        compiler_params=_params("parallel", "arbitrary"),
        name="ffn",
    )(x, g, g_out, w_in, w_in, w_out)


def _layer(x, w, p, l, sgu_tabs, conv_bufs, s0, *, prompt):
    h3, xn = _in_proj(x, p["g_mix"], w["w_in"], l, p["ln_v_g"])
    g3 = _gates(xn, w["w_gate"], l, p["b_gate"])
    ya, conv_new = _conv_branch(xn, w["w_in"], l, p["conv_w"], conv_bufs, prompt=prompt)
    ys, *state = _s5_branch(h3, p["s5"], s0, prompt=prompt)
    yb = _glu(ys, w["w_glu"], l, p["b_glu"])
    yc = _sgu_branch(h3, *sgu_tabs, l)
    x = _merge_project(ya, yb, yc, w["w_conv_out"], w["w_ssm_out"], w["w_sgu_out"], g3, w["w_o"], x, l)
    x = _ffn(x, p["g_ffn"], w["w_ffn_in"], w["w_ffn_out"], l, p["g_final"], norm_out=(l == DEPTH - 1))
    return x, conv_new, state, h3[2]


def kernel(x_prompt, x_sample, cache_conv, state_ssm_re, state_ssm_im, norm_mix_g, w_in, conv_w, w_conv_out, ssm_lam_re, ssm_lam_im, ssm_log_dt, ssm_b_re, ssm_b_im, ssm_c_re, ssm_c_im, ssm_d, w_glu, b_glu, w_ssm_out, ln_v_g, w_sgu_s, b_sgu_s, w_sgu_out, w_gate, b_gate, w_o, norm_ffn_g, w_ffn_in, w_ffn_out, norm_final_g):
    seq = x_prompt.shape[1]
    nb = seq // BLK
    xp = x_prompt.reshape(nb, SUBLANES, BLK // SUBLANES, D_MODEL).transpose(0, 2, 1, 3).reshape(seq, D_MODEL)
    xs = x_sample.transpose(1, 0, 2).reshape(BLK, D_MODEL)

    w = {
        "w_in": w_in.astype(BF16),
        "w_gate": w_gate.transpose(0, 2, 1, 3).astype(BF16),
        "w_glu": w_glu.astype(BF16),
        "w_conv_out": w_conv_out.astype(BF16),
        "w_ssm_out": w_ssm_out.astype(BF16),
        "w_sgu_out": w_sgu_out.astype(BF16),
        "w_o": w_o.astype(BF16),
        "w_ffn_in": w_ffn_in.astype(BF16),
        "w_ffn_out": w_ffn_out.astype(BF16),
    }
    sgu_p = _sgu_tables(w_sgu_s, b_sgu_s, BLK // SUBLANES, SUBLANES)
    sgu_s = _sgu_tables(w_sgu_s, b_sgu_s, 32, 32)

    conv_p, re_p, im_p, conv_s, re_s, im_s, v_s = [], [], [], [], [], [], []
    for l in range(DEPTH):
        p = {
            "g_mix": norm_mix_g[l].reshape(1, D_MODEL),
            "b_gate": b_gate[l].reshape(1, 3 * D_MODEL),
            "conv_w": conv_w[l],
            "s5": _s5_tables(ssm_lam_re[l], ssm_lam_im[l], ssm_log_dt[l], ssm_b_re[l], ssm_b_im[l],
                             ssm_c_re[l], ssm_c_im[l], ssm_d[l]),
            "b_glu": b_glu[l].reshape(1, W_BR),
            "ln_v_g": ln_v_g[l].reshape(1, W_BR),
            "g_ffn": norm_ffn_g[l].reshape(1, D_MODEL),
            "g_final": norm_final_g.reshape(1, D_MODEL),
        }
        xp, cp, sp, _ = _layer(xp, w, p, l, sgu_p, None, None, prompt=True)
        bufs = (cache_conv[l, :, 0, :], cache_conv[l, :, 1, :])
        s0 = (state_ssm_re[l].reshape(32, N_Q * Q_ST), state_ssm_im[l].reshape(32, N_Q * Q_ST))
        xs, cs, ss, vs = _layer(xs, w, p, l, sgu_s, bufs, s0, prompt=False)
        conv_p.append(cp[-1, 0:2].reshape(1, 2, W_BR))
        re_p.append(sp[0][:, 0, :].reshape(1, 64, 64))
        im_p.append(sp[0][:, 1, :].reshape(1, 64, 64))
        conv_s.append(cs.reshape(2, 32, W_BR).transpose(1, 0, 2))
        re_s.append(ss[0].reshape(32, 64, 64))
        im_s.append(ss[1].reshape(32, 64, 64))
        v_s.append(vs.reshape(32, 32, W_BR).transpose(1, 0, 2))

    yp, ys = xp, xs
    y_prompt = yp.reshape(nb, BLK // SUBLANES, SUBLANES, D_MODEL).transpose(0, 2, 1, 3).reshape(1, seq, D_MODEL)
    y_sample = ys.reshape(32, 32, D_MODEL).transpose(1, 0, 2)
    return (y_prompt, y_sample, jnp.stack(conv_p), jnp.stack(re_p), jnp.stack(im_p),
            jnp.stack(conv_s), jnp.stack(re_s), jnp.stack(im_s), jnp.stack(v_s))
```

```python
import functools

import jax
import jax.numpy as jnp
from jax import lax
from jax.experimental import pallas as pl
from jax.experimental.pallas import tpu as pltpu

F32 = jnp.float32
BF16 = jnp.bfloat16

D_MODEL = 2048
DEPTH = 4
W_BR = 1024
SSM_GROUPS = 64
N_Q = 4
Q_GROUPS = SSM_GROUPS // N_Q
Q_IN = Q_GROUPS * 16
Q_ST = Q_GROUPS * 64
SGU_HEADS = 8
HEAD_DIM = 128
D_FF = 5632
EPS = 1e-6
BLK = 1024
SUBLANES = 8
CONV_TC = 256
VMEM_LIMIT_BYTES = 56 * 1024 * 1024


def _params(*sem):
    return pltpu.CompilerParams(dimension_semantics=sem, vmem_limit_bytes=VMEM_LIMIT_BYTES)


def _gelu(x):
    return jax.nn.gelu(x, approximate=True)


def _rms(x, g):
    return x * lax.rsqrt(jnp.mean(x * x, axis=-1, keepdims=True) + EPS) * g


def _load_resident(first_step, pairs, sems):
    @pl.when(first_step)
    def _():
        copies = [pltpu.make_async_copy(src, dst, sems.at[k]) for k, (src, dst) in enumerate(pairs)]
        for c in copies:
            c.start()
        for c in copies:
            c.wait()


def _in_proj_kernel(x_ref, g_ref, w_hbm, lng_ref, h_ref, xn_ref, w_ref, sems, *, layer):
    w_src = w_hbm.at[layer, :, pl.ds(3 * W_BR, 3 * W_BR)]
    _load_resident(pl.program_id(0) == 0, [(w_src, w_ref)], sems)
    xn = _rms(x_ref[...], g_ref[...]).astype(BF16)
    xn_ref[...] = xn
    plane = lambda k: jnp.dot(xn, w_ref[:, k * W_BR:(k + 1) * W_BR], preferred_element_type=F32)
    h_ref[0] = plane(0)
    h_ref[1] = _gelu(plane(1))
    gv = _gelu(plane(2))
    xc = gv - jnp.mean(gv, axis=-1, keepdims=True)
    h_ref[2] = xc * lax.rsqrt(jnp.mean(xc * xc, axis=-1, keepdims=True) + EPS) * lng_ref[...]


def _in_proj(x, g, w_all, layer, ln_g):
    m = x.shape[0]
    tm = 512
    return pl.pallas_call(
        functools.partial(_in_proj_kernel, layer=layer),
        grid=(m // tm,),
        in_specs=[
            pl.BlockSpec((tm, D_MODEL), lambda i: (i, 0)),
            pl.BlockSpec((1, D_MODEL), lambda i: (0, 0)),
            pl.BlockSpec(memory_space=pl.ANY),
            pl.BlockSpec((1, W_BR), lambda i: (0, 0)),
        ],
        out_specs=[
            pl.BlockSpec((3, tm, W_BR), lambda i: (0, i, 0)),
            pl.BlockSpec((tm, D_MODEL), lambda i: (i, 0)),
        ],
        out_shape=[jax.ShapeDtypeStruct((3, m, W_BR), F32), jax.ShapeDtypeStruct((m, D_MODEL), BF16)],
        scratch_shapes=[pltpu.VMEM((D_MODEL, 3 * W_BR), BF16), pltpu.SemaphoreType.DMA((1,))],
        compiler_params=_params("arbitrary"),
        name="in_proj",
    )(x, g, w_all, ln_g)


def _conv_taps(z, b1, b2, w, r):
    prev1 = jnp.concatenate([b1, z[:BLK - r]], axis=0)
    prev2 = jnp.concatenate([b2, b1, z[:BLK - 2 * r]], axis=0)
    return w[0:1] * prev2 + w[1:2] * prev1 + w[2:3] * z


def _conv_split(xn_ref, wb_ref, wc_ref, wh_ref):
    xn = xn_ref[...]
    dot = lambda w: jnp.dot(xn, w[...], preferred_element_type=F32)
    return dot(wb_ref), dot(wc_ref) * dot(wh_ref)


def _conv_prompt_kernel(xn_ref, wb_ref, wc_ref, wh_ref, cw_ref, ya_ref, cn_ref, carry_ref):
    r = SUBLANES
    j = pl.program_id(1)

    @pl.when(pl.program_id(0) == 0)
    def _():
        carry_ref[j] = jnp.zeros(carry_ref.shape[1:], F32)

    b, z = _conv_split(xn_ref, wb_ref, wc_ref, wh_ref)
    last1 = z[BLK - r:]
    last2 = z[BLK - 2 * r:BLK - r]
    row = lax.broadcasted_iota(jnp.int32, last1.shape, 0)
    carry = carry_ref[j]
    b1 = jnp.where(row == 0, carry[1:2, :], pltpu.roll(last1, 1, 0))
    b2 = jnp.where(row == 0, carry[0:1, :], pltpu.roll(last2, 1, 0))
    ya_ref[...] = (b * _conv_taps(z, b1, b2, cw_ref[...], r)).astype(BF16)
    tail = jnp.where(row == 0, last2[r - 1:r], last1[r - 1:r])
    carry_ref[j] = tail
    cn_ref[...] = tail


def _conv_sample_kernel(xn_ref, wb_ref, wc_ref, wh_ref, cw_ref, buf0_ref, buf1_ref, ya_ref, cn_ref):
    r = 32
    b, z = _conv_split(xn_ref, wb_ref, wc_ref, wh_ref)
    ya_ref[...] = (b * _conv_taps(z, buf1_ref[...], buf0_ref[...], cw_ref[...], r)).astype(BF16)
    cn_ref[...] = z[BLK - 2 * r:]


def _conv_branch(xn, w, layer, conv_w, bufs, *, prompt):
    m = xn.shape[0]
    tc = CONV_TC
    nt = W_BR // tc
    w_spec = lambda k: pl.BlockSpec((None, D_MODEL, tc), lambda i, j, k=k: (layer, 0, k * nt + j))
    in_specs = [
        pl.BlockSpec((BLK, D_MODEL), lambda i, j: (i, 0)),
        w_spec(0), w_spec(1), w_spec(2),
        pl.BlockSpec((3, tc), lambda i, j: (0, j)),
    ]
    ya_spec = pl.BlockSpec((BLK, tc), lambda i, j: (i, j))
    if prompt:
        return pl.pallas_call(
            _conv_prompt_kernel,
            grid=(m // BLK, nt),
            in_specs=in_specs,
            out_specs=[ya_spec, pl.BlockSpec((None, SUBLANES, tc), lambda i, j: (i, 0, j))],
            out_shape=[jax.ShapeDtypeStruct((m, W_BR), BF16),
                       jax.ShapeDtypeStruct((m // BLK, SUBLANES, W_BR), F32)],
            scratch_shapes=[pltpu.VMEM((nt, SUBLANES, tc), F32)],
            compiler_params=_params("arbitrary", "arbitrary"),
            name="conv_prompt",
        )(xn, w, w, w, conv_w)
    buf_spec = pl.BlockSpec((32, tc), lambda i, j: (0, j))
    return pl.pallas_call(
        _conv_sample_kernel,
        grid=(1, nt),
        in_specs=in_specs + [buf_spec, buf_spec],
        out_specs=[ya_spec, pl.BlockSpec((64, tc), lambda i, j: (0, j))],
        out_shape=[jax.ShapeDtypeStruct((m, W_BR), BF16), jax.ShapeDtypeStruct((64, W_BR), F32)],
        compiler_params=_params("arbitrary", "arbitrary"),
        name="conv_sample",
    )(xn, w, w, w, conv_w, bufs[0], bufs[1])


def _scan_rows(hre, him, lr, li, init, base, stride, steps):
    def step(j, st):
        sr, si = st
        r0 = pl.multiple_of(base + j * stride, SUBLANES)
        nr = lr * sr - li * si + hre[pl.ds(r0, SUBLANES), :]
        ni = lr * si + li * sr + him[pl.ds(r0, SUBLANES), :]
        hre[pl.ds(r0, SUBLANES), :] = nr
        him[pl.ds(r0, SUBLANES), :] = ni
        return nr, ni

    return lax.fori_loop(0, steps, step, init, unroll=4)


def _s5_project_in(u_ref, wb_ref, hre, him):
    for rows in (slice(0, BLK // 2), slice(BLK // 2, BLK)):
        bu = jnp.dot(u_ref[rows, :].astype(BF16), wb_ref[...], preferred_element_type=F32)
        hre[rows, :] = bu[:, :Q_ST]
        him[rows, :] = bu[:, Q_ST:]


def _s5_project_out(u_ref, wc_ref, d_ref, y_ref, hre, him):
    for rows in (slice(0, BLK // 2), slice(BLK // 2, BLK)):
        h = jnp.concatenate([hre[rows, :], him[rows, :]], axis=1).astype(BF16)
        y_ref[rows, :] = jnp.dot(h, wc_ref[...], preferred_element_type=F32) + d_ref[...] * u_ref[rows, :]


def _s5_prompt_kernel(u_ref, wb_ref, lam_ref, lams_ref, wc_ref, d_ref, y_ref, sfin_ref, hre, him, carry):
    @pl.when(pl.program_id(1) == 0)
    def _():
        carry[...] = jnp.zeros_like(carry)

    _s5_project_in(u_ref, wb_ref, hre, him)
    tile = (SUBLANES, Q_ST)
    lr = jnp.broadcast_to(lam_ref[0:1, :], tile)
    li = jnp.broadcast_to(lam_ref[1:2, :], tile)
    zero = jnp.zeros(tile, F32)
    steps = BLK // SUBLANES
    er, ei = _scan_rows(hre, him, lr, li, (zero, zero), 0, SUBLANES, steps)
    pr, pi = lams_ref[0:1, :], lams_ref[1:2, :]
    cr, ci = carry[0:1, :], carry[1:2, :]
    row = lax.broadcasted_iota(jnp.int32, tile, 0)
    sin_r, sin_i = zero, zero
    for s in range(SUBLANES):
        sin_r = jnp.where(row == s, cr, sin_r)
        sin_i = jnp.where(row == s, ci, sin_i)
        cr, ci = er[s:s + 1] + pr * cr - pi * ci, ei[s:s + 1] + pr * ci + pi * cr
    carry[0:1, :] = cr
    carry[1:2, :] = ci
    sfin_ref[0:1, :] = cr
    sfin_ref[1:2, :] = ci

    def fix(j, e):
        fr, fi = e
        nr = lr * fr - li * fi
        ni = lr * fi + li * fr
        r0 = pl.multiple_of(j * SUBLANES, SUBLANES)
        hre[pl.ds(r0, SUBLANES), :] += nr
        him[pl.ds(r0, SUBLANES), :] += ni
        return nr, ni

    lax.fori_loop(0, steps, fix, (sin_r, sin_i), unroll=4)
    _s5_project_out(u_ref, wc_ref, d_ref, y_ref, hre, him)


def _s5_sample_kernel(u_ref, wb_ref, lam_ref, wc_ref, d_ref, s0r_ref, s0i_ref, y_ref, sr_ref, si_ref, hre, him):
    _s5_project_in(u_ref, wb_ref, hre, him)
    tile = (SUBLANES, Q_ST)
    lr = jnp.broadcast_to(lam_ref[0:1, :], tile)
    li = jnp.broadcast_to(lam_ref[1:2, :], tile)
    for r in range(32 // SUBLANES):
        rows = slice(r * SUBLANES, (r + 1) * SUBLANES)
        er, ei = _scan_rows(hre, him, lr, li, (s0r_ref[rows, :], s0i_ref[rows, :]), r * SUBLANES, 32, 32)
        sr_ref[rows, :] = er
        si_ref[rows, :] = ei
    _s5_project_out(u_ref, wc_ref, d_ref, y_ref, hre, him)


def _s5_branch(h3, tabs, s0, *, prompt):
    m = h3.shape[1]
    u_spec = pl.BlockSpec((None, BLK, Q_IN), lambda q, i: (0, i, q))
    wb_spec = pl.BlockSpec((None, Q_IN, 2 * Q_ST), lambda q, i: (q, 0, 0))
    lam_spec = pl.BlockSpec((None, 2, Q_ST), lambda q, i: (q, 0, 0))
    wc_spec = pl.BlockSpec((None, 2 * Q_ST, Q_IN), lambda q, i: (q, 0, 0))
    d_spec = pl.BlockSpec((1, Q_IN), lambda q, i: (0, q))
    y_spec = pl.BlockSpec((BLK, Q_IN), lambda q, i: (i, q))
    scratch = [pltpu.VMEM((BLK, Q_ST), F32), pltpu.VMEM((BLK, Q_ST), F32)]
    if prompt:
        return pl.pallas_call(
            _s5_prompt_kernel,
            grid=(N_Q, m // BLK),
            in_specs=[u_spec, wb_spec, lam_spec, lam_spec, wc_spec, d_spec],
            out_specs=[y_spec, pl.BlockSpec((None, 2, Q_ST), lambda q, i: (q, 0, 0))],
            out_shape=[jax.ShapeDtypeStruct((m, W_BR), F32), jax.ShapeDtypeStruct((N_Q, 2, Q_ST), F32)],
            scratch_shapes=scratch + [pltpu.VMEM((SUBLANES, Q_ST), F32)],
            compiler_params=_params("parallel", "arbitrary"),
            name="s5_prompt",
        )(h3, tabs["wb"], tabs["lam"], tabs["lam_seg"], tabs["wc"], tabs["d"])
    st_spec = pl.BlockSpec((32, Q_ST), lambda q, i: (0, q))
    return pl.pallas_call(
        _s5_sample_kernel,
        grid=(N_Q, 1),
        in_specs=[u_spec, wb_spec, lam_spec, wc_spec, d_spec, st_spec, st_spec],
        out_specs=[y_spec, st_spec, st_spec],
        out_shape=[jax.ShapeDtypeStruct((m, W_BR), F32)] + [jax.ShapeDtypeStruct((32, N_Q * Q_ST), F32)] * 2,
        scratch_shapes=scratch,
        compiler_params=_params("parallel", "arbitrary"),
        name="s5_sample",
    )(h3, tabs["wb"], tabs["lam"], tabs["wc"], tabs["d"], s0[0], s0[1])


def _s5_tables(lam_re, lam_im, log_dt, b_re, b_im, c_re, c_im, d):
    dt = jnp.exp(log_dt)[:, None]
    mag = jnp.exp(lam_re * dt)
    lbr, lbi = mag * jnp.cos(lam_im * dt), mag * jnp.sin(lam_im * dt)
    den = lam_re * lam_re + lam_im * lam_im
    qr = ((lbr - 1.0) * lam_re + lbi * lam_im) / den
    qi = (lbi * lam_re - (lbr - 1.0) * lam_im) / den
    bbr = qr[:, :, None] * b_re - qi[:, :, None] * b_im
    bbi = qr[:, :, None] * b_im + qi[:, :, None] * b_re
    eye = jnp.eye(Q_GROUPS, dtype=F32)

    def block_in(b):
        return jnp.einsum("ab,qapi->qaibp", eye, b.reshape(N_Q, Q_GROUPS, 64, 16)).reshape(N_Q, Q_IN, Q_ST)

    def block_out(c):
        return jnp.einsum("ab,qaip->qapbi", eye, c.reshape(N_Q, Q_GROUPS, 16, 64)).reshape(N_Q, Q_ST, Q_IN)

    wb = jnp.concatenate([block_in(bbr), block_in(bbi)], axis=2).astype(BF16)
    wc = jnp.concatenate([block_out(c_re), -block_out(c_im)], axis=1).astype(BF16)
    sr, si = lbr, lbi
    for _ in range(7):
        sr, si = sr * sr - si * si, 2.0 * sr * si
    pack = lambda a, b: jnp.stack([a.reshape(N_Q, Q_ST), b.reshape(N_Q, Q_ST)], axis=1)
    return {"wb": wb, "wc": wc, "lam": pack(lbr, lbi), "lam_seg": pack(sr, si), "d": d.reshape(1, W_BR)}


def _sgu_kernel(k_hbm, v_ref, u_ref, bs_ref, yc_ref, k_ref, sems, *, layer):
    _load_resident(pl.program_id(0) == 0, [(k_hbm.at[layer], k_ref)], sems)
    for hd in range(SGU_HEADS):
        cols = slice(hd * HEAD_DIM, (hd + 1) * HEAD_DIM)
        mix = jnp.dot(k_ref[hd], v_ref[:, cols].astype(BF16), preferred_element_type=F32) + bs_ref[:, hd:hd + 1]
        yc_ref[:, cols] = (u_ref[:, cols] * mix).astype(BF16)


def _sgu_branch(h3, kmat_all, bias_all, layer):
    m = h3.shape[1]
    plane = lambda k: pl.BlockSpec((None, BLK, W_BR), lambda i, k=k: (k, i, 0))
    return pl.pallas_call(
        functools.partial(_sgu_kernel, layer=layer),
        grid=(m // BLK,),
        in_specs=[
            pl.BlockSpec(memory_space=pl.ANY),
            plane(2),
            plane(1),
            pl.BlockSpec((None, BLK, SGU_HEADS), lambda i: (layer, 0, 0)),
        ],
        out_specs=pl.BlockSpec((BLK, W_BR), lambda i: (i, 0)),
        out_shape=jax.ShapeDtypeStruct((m, W_BR), BF16),
        scratch_shapes=[pltpu.VMEM((SGU_HEADS, BLK, BLK), BF16), pltpu.SemaphoreType.DMA((1,))],
        compiler_params=_params("arbitrary"),
        name="sgu",
    )(kmat_all, h3, h3, bias_all)


def _sgu_tables(w_s, b_s, steps, width):
    w = jnp.where(jnp.tril(jnp.ones((128, 128), dtype=bool)), w_s, 0)[:, :, :steps, :steps]
    row = jnp.arange(BLK)
    expand = (row[:, None] // width == jnp.arange(steps)[None, :]).astype(F32)
    same_slot = row[:, None] % width == row[None, :] % width
    spread = jnp.einsum("rj,lhjk,ck->lhrc", expand, w, expand)
    kmat = jnp.where(same_slot, spread, 0).astype(BF16)
    bias = jnp.repeat(b_s[:, :, :steps].transpose(0, 2, 1), width, axis=1)
    return kmat, bias


def _gates_kernel(xn_ref, w_ref, b_ref, o_ref):
    acc = jnp.dot(xn_ref[...], w_ref[...], preferred_element_type=F32) + b_ref[...]
    o_ref[...] = jax.nn.sigmoid(acc).astype(BF16)


def _gates(xn, w_all, layer, b):
    m = xn.shape[0]
    tn = MERGE_TN
    per = D_MODEL // tn
    return pl.pallas_call(
        _gates_kernel,
        grid=(3 * per, m // BLK),
        in_specs=[
            pl.BlockSpec((BLK, D_MODEL), lambda j, i: (i, 0)),
            pl.BlockSpec((None, None, D_MODEL, tn), lambda j, i: (layer, j // per, 0, j % per)),
            pl.BlockSpec((1, tn), lambda j, i: (0, j)),
        ],
        out_specs=pl.BlockSpec((None, None, BLK, tn), lambda j, i: (j // per, j % per, i, 0)),
        out_shape=jax.ShapeDtypeStruct((3, per, m, tn), BF16),
        compiler_params=_params("parallel", "parallel"),
        name="gates",
    )(xn, w_all, b)


def _glu_kernel(ys_ref, w_ref, b_ref, o_ref):
    g = _gelu(ys_ref[...])
    gate = jax.nn.sigmoid(jnp.dot(g.astype(BF16), w_ref[...], preferred_element_type=F32) + b_ref[...])
    o_ref[...] = (g * gate).astype(BF16)


def _glu(ys, w_all, layer, b):
    m = ys.shape[0]
    return pl.pallas_call(
        _glu_kernel,
        grid=(m // BLK,),
        in_specs=[
            pl.BlockSpec((BLK, W_BR), lambda i: (i, 0)),
            pl.BlockSpec((None, W_BR, W_BR), lambda i: (layer, 0, 0)),
            pl.BlockSpec((1, W_BR), lambda i: (0, 0)),
        ],
        out_specs=pl.BlockSpec((BLK, W_BR), lambda i: (i, 0)),
        out_shape=jax.ShapeDtypeStruct((m, W_BR), BF16),
        compiler_params=_params("parallel"),
        name="glu",
    )(ys, w_all, b)


MERGE_TM = 512
MERGE_TN = 1024
MERGE_NT = D_MODEL // MERGE_TN


def _merge_project_kernel(ya_ref, yb_ref, yc_ref, wa_hbm, wb_hbm, wc_hbm, g0_ref, g1_ref, g2_ref,
                          wo_hbm, x_ref, o_ref, merged_ref, wa_ref, wb_ref, wc_ref, wo_ref, sems, *, layer):
    j = pl.program_id(1)
    _load_resident((pl.program_id(0) == 0) & (j == 0),
                   [(wa_hbm.at[layer], wa_ref), (wb_hbm.at[layer], wb_ref), (wc_hbm.at[layer], wc_ref),
                    (wo_hbm.at[layer], wo_ref)], sems)

    for t in range(MERGE_NT):
        cols = slice(t * MERGE_TN, (t + 1) * MERGE_TN)

        @pl.when(j == t)
        def _():
            dot = lambda a, w: jnp.dot(a[...], w[:, cols], preferred_element_type=F32)
            merged = (g0_ref[...] * dot(ya_ref, wa_ref) + g1_ref[...] * dot(yb_ref, wb_ref)
                      + g2_ref[...] * dot(yc_ref, wc_ref))
            merged_ref[:, cols] = merged.astype(BF16)

        @pl.when(j == MERGE_NT + t)
        def _():
            o_ref[...] = x_ref[...] + jnp.dot(merged_ref[...], wo_ref[:, cols], preferred_element_type=F32)


def _merge_project(ya, yb, yc, wa, wb, wc, g3, wo, x, layer):
    m = ya.shape[0]
    tm, tn, nt = MERGE_TM, MERGE_TN, MERGE_NT
    first = lambda j: jnp.minimum(j, nt - 1)
    second = lambda j: jnp.maximum(j - nt, 0)
    y_spec = pl.BlockSpec((tm, W_BR), lambda i, j: (i, 0))
    hbm = pl.BlockSpec(memory_space=pl.ANY)
    gate = lambda k: pl.BlockSpec((None, None, tm, tn), lambda i, j, k=k: (k, first(j), i, 0))
    branch_w = pltpu.VMEM((W_BR, D_MODEL), BF16)
    return pl.pallas_call(
        functools.partial(_merge_project_kernel, layer=layer),
        grid=(m // tm, 2 * nt),
        in_specs=[y_spec, y_spec, y_spec, hbm, hbm, hbm, gate(0), gate(1), gate(2), hbm,
                  pl.BlockSpec((tm, tn), lambda i, j: (i, second(j)))],
        out_specs=pl.BlockSpec((tm, tn), lambda i, j: (i, second(j))),
        out_shape=jax.ShapeDtypeStruct((m, D_MODEL), F32),
        scratch_shapes=[pltpu.VMEM((tm, D_MODEL), BF16), branch_w, branch_w, branch_w,
                        pltpu.VMEM((D_MODEL, D_MODEL), BF16), pltpu.SemaphoreType.DMA((4,))],
        compiler_params=_params("arbitrary", "arbitrary"),
        name="merge_project",
    )(ya, yb, yc, wa, wb, wc, g3, g3, g3, wo, x)


def _ffn_kernel(x_ref, g_ref, gout_ref, wg_ref, wu_ref, wo_ref, o_ref, xn_ref, *, norm_out):
    j = pl.program_id(1)

    @pl.when(j == 0)
    def _():
        x = x_ref[...]
        xn_ref[...] = _rms(x, g_ref[...]).astype(BF16)
        o_ref[...] = x

    xn = xn_ref[...]
    gate = jnp.dot(xn, wg_ref[...], preferred_element_type=F32)
    up = jnp.dot(xn, wu_ref[...], preferred_element_type=F32)
    act = (jax.nn.silu(gate) * up).astype(BF16)
    o_ref[...] += jnp.dot(act, wo_ref[...], preferred_element_type=F32)

    if norm_out:
        @pl.when(j == pl.num_programs(1) - 1)
        def _():
            o_ref[...] = _rms(o_ref[...], gout_ref[...])


def _ffn(x, g, w_in, w_out, layer, g_out, *, norm_out):
    m = x.shape[0]
    tm, tf = 1024, 512
    nf = D_FF // tf
    return pl.pallas_call(
        functools.partial(_ffn_kernel, norm_out=norm_out),
        grid=(m // tm, nf),
        in_specs=[
            pl.BlockSpec((tm, D_MODEL), lambda i, j: (i, 0)),
            pl.BlockSpec((1, D_MODEL), lambda i, j: (0, 0)),
            pl.BlockSpec((1, D_MODEL), lambda i, j: (0, 0)),
            pl.BlockSpec((None, D_MODEL, tf), lambda i, j: (layer, 0, j)),
            pl.BlockSpec((None, D_MODEL, tf), lambda i, j: (layer, 0, j + nf)),
            pl.BlockSpec((None, tf, D_MODEL), lambda i, j: (layer, j, 0)),
        ],
        out_specs=pl.BlockSpec((tm, D_MODEL), lambda i, j: (i, 0)),
        out_shape=jax.ShapeDtypeStruct((m, D_MODEL), F32),
        scratch_shapes=[pltpu.VMEM((tm, D_MODEL), BF16)],
        compiler_params=_params("parallel", "arbitrary"),
        name="ffn",
    )(x, g, g_out, w_in, w_in, w_out)


def _layer(x, w, p, l, sgu_tabs, conv_bufs, s0, *, prompt):
    h3, xn = _in_proj(x, p["g_mix"], w["w_in"], l, p["ln_v_g"])
    g3 = _gates(xn, w["w_gate"], l, p["b_gate"])
    ya, conv_new = _conv_branch(xn, w["w_in"], l, p["conv_w"], conv_bufs, prompt=prompt)
    ys, *state = _s5_branch(h3, p["s5"], s0, prompt=prompt)
    yb = _glu(ys, w["w_glu"], l, p["b_glu"])
    yc = _sgu_branch(h3, *sgu_tabs, l)
    x = _merge_project(ya, yb, yc, w["w_conv_out"], w["w_ssm_out"], w["w_sgu_out"], g3, w["w_o"], x, l)
    x = _ffn(x, p["g_ffn"], w["w_ffn_in"], w["w_ffn_out"], l, p["g_final"], norm_out=(l == DEPTH - 1))
    return x, conv_new, state, h3[2]


def kernel(x_prompt, x_sample, cache_conv, state_ssm_re, state_ssm_im, norm_mix_g, w_in, conv_w, w_conv_out, ssm_lam_re, ssm_lam_im, ssm_log_dt, ssm_b_re, ssm_b_im, ssm_c_re, ssm_c_im, ssm_d, w_glu, b_glu, w_ssm_out, ln_v_g, w_sgu_s, b_sgu_s, w_sgu_out, w_gate, b_gate, w_o, norm_ffn_g, w_ffn_in, w_ffn_out, norm_final_g):
    seq = x_prompt.shape[1]
    nb = seq // BLK
    xp = x_prompt.reshape(nb, SUBLANES, BLK // SUBLANES, D_MODEL).transpose(0, 2, 1, 3).reshape(seq, D_MODEL)
    xs = x_sample.transpose(1, 0, 2).reshape(BLK, D_MODEL)

    w = {
        "w_in": w_in.astype(BF16),
        "w_gate": w_gate.transpose(0, 2, 1, 3).astype(BF16),
        "w_glu": w_glu.astype(BF16),
        "w_conv_out": w_conv_out.astype(BF16),
        "w_ssm_out": w_ssm_out.astype(BF16),
        "w_sgu_out": w_sgu_out.astype(BF16),
        "w_o": w_o.astype(BF16),
        "w_ffn_in": w_ffn_in.astype(BF16),
        "w_ffn_out": w_ffn_out.astype(BF16),
    }
    sgu_p = _sgu_tables(w_sgu_s, b_sgu_s, BLK // SUBLANES, SUBLANES)
    sgu_s = _sgu_tables(w_sgu_s, b_sgu_s, 32, 32)

    conv_p, re_p, im_p, conv_s, re_s, im_s, v_s = [], [], [], [], [], [], []
    for l in range(DEPTH):
        p = {
            "g_mix": norm_mix_g[l].reshape(1, D_MODEL),
            "b_gate": b_gate[l].reshape(1, 3 * D_MODEL),
            "conv_w": conv_w[l],
            "s5": _s5_tables(ssm_lam_re[l], ssm_lam_im[l], ssm_log_dt[l], ssm_b_re[l], ssm_b_im[l],
                             ssm_c_re[l], ssm_c_im[l], ssm_d[l]),
            "b_glu": b_glu[l].reshape(1, W_BR),
            "ln_v_g": ln_v_g[l].reshape(1, W_BR),
            "g_ffn": norm_ffn_g[l].reshape(1, D_MODEL),
            "g_final": norm_final_g.reshape(1, D_MODEL),
        }
        xp, cp, sp, _ = _layer(xp, w, p, l, sgu_p, None, None, prompt=True)
        bufs = (cache_conv[l, :, 0, :], cache_conv[l, :, 1, :])
        s0 = (state_ssm_re[l].reshape(32, N_Q * Q_ST), state_ssm_im[l].reshape(32, N_Q * Q_ST))
        xs, cs, ss, vs = _layer(xs, w, p, l, sgu_s, bufs, s0, prompt=False)
        conv_p.append(cp[-1, 0:2].reshape(1, 2, W_BR))
        re_p.append(sp[0][:, 0, :].reshape(1, 64, 64))
        im_p.append(sp[0][:, 1, :].reshape(1, 64, 64))
        conv_s.append(cs.reshape(2, 32, W_BR).transpose(1, 0, 2))
        re_s.append(ss[0].reshape(32, 64, 64))
        im_s.append(ss[1].reshape(32, 64, 64))
        v_s.append(vs.reshape(32, 32, W_BR).transpose(1, 0, 2))

    yp, ys = xp, xs
    y_prompt = yp.reshape(nb, BLK // SUBLANES, SUBLANES, D_MODEL).transpose(0, 2, 1, 3).reshape(1, seq, D_MODEL)
    y_sample = ys.reshape(32, 32, D_MODEL).transpose(1, 0, 2)
    return (y_prompt, y_sample, jnp.stack(conv_p), jnp.stack(re_p), jnp.stack(im_p),
            jnp.stack(conv_s), jnp.stack(re_s), jnp.stack(im_s), jnp.stack(v_s))
```
